```python
import math
import jax, jax.numpy as jnp
from jax import lax
import numpy as np

D_MODEL = 2048
BATCH = 16
SEQ = 256
DEPTH = 2
DEC_BATCH = 8
DEC_SEQ = 2048
PAST_LEN = 512

GRID_W = 64
N_MIXERS = 2
N_DN_LAYERS = (DEPTH + N_MIXERS - 1) // N_MIXERS
N_POOL_LAYERS = DEPTH // N_MIXERS
DN_DK = 128
DN_DV = 128
DN_NK = D_MODEL // DN_DK
DN_NV = 2 * DN_NK
DN_QK = DN_NK * DN_DK
DN_V = DN_NV * DN_DV
DN_CONV = 5
DN_CHUNK = 64
DN_IN = 2 * DN_QK + 2 * DN_V + 4 * DN_NV
POOL_WINDOWS = (2, 4, 8, 16)
N_POOL_GROUPS = len(POOL_WINDOWS)
POOL_GROUP = D_MODEL // N_POOL_GROUPS
N_EXPERTS = 64
TOP_K = 8
N_EXPERT_GROUPS = 8
TOPK_GROUPS = 4
D_EXPERT = 512
D_SHARED = 512
ROUTED_SCALE = 2.5
MOE_BLOCK = 128
EPS = 1e-6

kernel_name = 'hybrid_deltanet_pool_moe_flow_step'


def rmsnorm(x, g):
    xf = x.astype(jnp.float32)
    return xf * lax.rsqrt(jnp.mean(xf * xf, axis=-1, keepdims=True) + EPS) * g.astype(jnp.float32)


def adaln(cvec, w, b):
    m = jax.nn.silu(cvec.astype(jnp.float32)) @ w.astype(jnp.float32) + b.astype(jnp.float32)
    return jnp.split(m[:, None, :], 6, axis=-1)


def l2norm(x):
    return x * lax.rsqrt(jnp.sum(x * x, axis=-1, keepdims=True) + EPS)


def centred_depthwise_conv(x, w):
    k, ch = w.shape
    return lax.conv_general_dilated(x, w[:, None, :].astype(x.dtype), window_strides=(1,),
                                    padding=[(k // 2, k // 2)],
                                    dimension_numbers=('NWC', 'WIO', 'NWC'),
                                    feature_group_count=ch)


def chunk_gated_delta(q, k, v, g, beta, s0):
    b, h, t, dk = q.shape
    dv = v.shape[-1]
    c = DN_CHUNK
    n = t // c
    q = q.reshape(b, h, n, c, dk)
    k = k.reshape(b, h, n, c, dk)
    v = v.reshape(b, h, n, c, dv)
    gc = jnp.cumsum(g.reshape(b, h, n, c), axis=-1)
    beta = beta.reshape(b, h, n, c)
    tri = np.tril(np.ones((c, c), dtype=bool))
    strict = np.tril(np.ones((c, c), dtype=bool), -1)
    decay = jnp.exp(jnp.where(tri, gc[..., :, None] - gc[..., None, :], -jnp.inf))
    kb = k * beta[..., None]
    vb = v * beta[..., None]
    lmat = jnp.where(strict, jnp.einsum('bhncd,bhnsd->bhncs', kb, k) * decay, 0.0)
    u = lax.linalg.triangular_solve(lmat, vb, left_side=True, lower=True, unit_diagonal=True)
    w = lax.linalg.triangular_solve(lmat, kb * jnp.exp(gc)[..., None], left_side=True, lower=True,
                                    unit_diagonal=True)
    attn = jnp.einsum('bhncd,bhnsd->bhncs', q, k) * decay
    g_last = gc[..., -1]
    qg = q * jnp.exp(gc)[..., None]
    kd = k * jnp.exp(g_last[..., None] - gc)[..., None]

    def step(s, inp):
        qg_c, w_c, u_c, a_c, kd_c, gl_c = inp
        v_new = u_c - jnp.einsum('bhcd,bhde->bhce', w_c, s)
        o_c = jnp.einsum('bhcd,bhde->bhce', qg_c, s) + jnp.einsum('bhcs,bhse->bhce', a_c, v_new)
        s = s * jnp.exp(gl_c)[..., None, None] + jnp.einsum('bhcd,bhce->bhde', kd_c, v_new)
        return s, o_c

    xs = tuple(jnp.moveaxis(a, 2, 0) for a in (qg, w, u, attn, kd, g_last))
    s_fin, o = lax.scan(step, s0.astype(jnp.float32), xs)
    o = jnp.moveaxis(o, 0, 2).reshape(b, h, t, dv)
    return o, s_fin


def deltanet_mixer(h, s_init, w_in, conv_w, a_log, dt_bias, onorm_g, w_out):
    b, t, _ = h.shape
    proj = h @ w_in
    qkv, z, ab = jnp.split(proj, [2 * DN_QK + DN_V, 2 * DN_QK + 2 * DN_V], axis=-1)
    qkv = jax.nn.silu(centred_depthwise_conv(qkv, conv_w)).astype(jnp.float32)
    q, k, v = jnp.split(qkv, [DN_QK, 2 * DN_QK], axis=-1)
    rep = DN_NV // DN_NK
    q = jnp.repeat(l2norm(q.reshape(b, t, DN_NK, DN_DK)), rep, axis=2) * (DN_DK ** -0.5)
    k = jnp.repeat(l2norm(k.reshape(b, t, DN_NK, DN_DK)), rep, axis=2)
    v = v.reshape(b, t, DN_NV, DN_DV)
    ab = ab.astype(jnp.float32).reshape(b, t, 2, 2, DN_NV)
    g = -jnp.exp(a_log.astype(jnp.float32)) * jax.nn.softplus(ab[:, :, :, 0] + dt_bias.astype(jnp.float32))
    beta = jax.nn.sigmoid(ab[:, :, :, 1])
    qh, kh, vh = (a.transpose(0, 2, 1, 3) for a in (q, k, v))
    gt = g.transpose(0, 2, 3, 1)
    bt = beta.transpose(0, 2, 3, 1)
    flip = lambda a: jnp.flip(a, axis=2)
    o_f, s_f = chunk_gated_delta(qh, kh, vh, gt[:, 0], bt[:, 0], s_init[:, 0])
    o_b, s_b = chunk_gated_delta(flip(qh), flip(kh), flip(vh), flip(gt[:, 1]), flip(bt[:, 1]), s_init[:, 1])
    o = (o_f + flip(o_b)).transpose(0, 2, 1, 3)
    o = rmsnorm(o, onorm_g) * jax.nn.silu(z.reshape(b, t, DN_NV, DN_DV).astype(jnp.float32))
    return o.reshape(b, t, DN_V) @ w_out, jnp.stack([s_f, s_b], axis=1)


def window_bounds(n, w):
    t = np.arange(n)
    lo = np.maximum(t - w // 2, 0)
    hi = np.minimum(t + (w - w // 2) - 1, n - 1)
    return lo, hi


def pool_sequence(h):
    b, t, d = h.shape
    outs = []
    for gi, w in enumerate(POOL_WINDOWS):
        xg = h[..., gi * POOL_GROUP:(gi + 1) * POOL_GROUP]
        cs = jnp.pad(jnp.cumsum(xg, axis=1), ((0, 0), (1, 0), (0, 0)))
        lo, hi = window_bounds(t, w)
        cnt = (hi - lo + 1).astype(np.float32)
        outs.append((cs[:, hi + 1] - cs[:, lo]) / cnt[None, :, None])
    return jnp.concatenate(outs, axis=-1) - h


def pool_grid(h):
    b, t, d = h.shape
    rows = t // GRID_W
    x = h.reshape(b, rows, GRID_W, d)
    outs = []
    for gi, w in enumerate(POOL_WINDOWS):
        xg = x[..., gi * POOL_GROUP:(gi + 1) * POOL_GROUP]
        sat = jnp.pad(jnp.cumsum(jnp.cumsum(xg, axis=1), axis=2), ((0, 0), (1, 0), (1, 0), (0, 0)))
        rlo, rhi = window_bounds(rows, w)
        clo, chi = window_bounds(GRID_W, w)
        top = sat[:, rhi + 1]
        bot = sat[:, rlo]
        box = top[:, :, chi + 1] - top[:, :, clo] - bot[:, :, chi + 1] + bot[:, :, clo]
        cnt = np.outer(rhi - rlo + 1, chi - clo + 1).astype(np.float32)
        outs.append(box / cnt[None, :, :, None])
    return (jnp.concatenate(outs, axis=-1) - x).reshape(b, t, d)


def pool_mixer(h, latent, w_grp, scale):
    p = pool_grid(h) if latent else pool_sequence(h)
    b, t, d = p.shape
    y = jnp.einsum('btgc,gce->btge', p.reshape(b, t, N_POOL_GROUPS, POOL_GROUP), w_grp)
    return y.reshape(b, t, d) * scale


def routed_experts(x, idx, wts, w_gu, w_down):
    n, d = x.shape
    a = n * TOP_K
    flat_e = idx.reshape(-1)
    order = jnp.argsort(flat_e)
    e_sorted = flat_e[order]
    tok_sorted = (order // TOP_K).astype(jnp.int32)
    w_sorted = wts.reshape(-1)[order]
    counts = jnp.bincount(flat_e, length=N_EXPERTS)
    padded = (counts + MOE_BLOCK - 1) // MOE_BLOCK * MOE_BLOCK
    start = jnp.cumsum(counts) - counts
    pend = jnp.cumsum(padded)
    pstart = pend - padded
    dest = pstart[e_sorted] + jnp.arange(a, dtype=jnp.int32) - start[e_sorted]
    n_blocks = (a + N_EXPERTS * (MOE_BLOCK - 1) + MOE_BLOCK - 1) // MOE_BLOCK
    p = n_blocks * MOE_BLOCK
    buf_tok = jnp.zeros((p,), jnp.int32).at[dest].set(tok_sorted)
    buf_w = jnp.zeros((p,), jnp.float32).at[dest].set(w_sorted)
    block_start = jnp.arange(n_blocks, dtype=jnp.int32) * MOE_BLOCK
    block_e = jnp.minimum(jnp.sum(pend[None, :] <= block_start[:, None], axis=1), N_EXPERTS - 1)

    def body(blk, acc):
        toks = lax.dynamic_slice(buf_tok, (blk * MOE_BLOCK,), (MOE_BLOCK,))
        gw = lax.dynamic_slice(buf_w, (blk * MOE_BLOCK,), (MOE_BLOCK,))
        e = block_e[blk]
        xb = x[toks]
        gate, up = jnp.split(xb @ w_gu[e], 2, axis=-1)
        yb = (jax.nn.silu(gate) * up) @ w_down[e]
        return acc.at[toks].add((yb * gw[:, None]).astype(jnp.float32))

    return lax.fori_loop(0, n_blocks, body, jnp.zeros((n, d), jnp.float32))


def moe_ffn(h, router_w, router_b, w_gu, w_down, sh_gu, sh_down):
    b, t, d = h.shape
    x = h.reshape(b * t, d)
    scores = jax.nn.sigmoid((x @ router_w).astype(jnp.float32))
    choice = scores + router_b.astype(jnp.float32)
    grouped = choice.reshape(-1, N_EXPERT_GROUPS, N_EXPERTS // N_EXPERT_GROUPS)
    group_score = jnp.sum(lax.top_k(grouped, 2)[0], axis=-1)
    _, gidx = lax.top_k(group_score, TOPK_GROUPS)
    gmask = jnp.sum(jax.nn.one_hot(gidx, N_EXPERT_GROUPS, dtype=jnp.float32), axis=1) > 0
    emask = jnp.repeat(gmask, N_EXPERTS // N_EXPERT_GROUPS, axis=1)
    _, idx = lax.top_k(jnp.where(emask, choice, -jnp.inf), TOP_K)
    wts = jnp.take_along_axis(scores, idx, axis=1)
    wts = wts / jnp.sum(wts, axis=-1, keepdims=True) * ROUTED_SCALE
    routed = routed_experts(x, idx, wts, w_gu, w_down)
    gs, us = jnp.split(x @ sh_gu, 2, axis=-1)
    shared = (jax.nn.silu(gs) * us) @ sh_down
    return (routed + shared).reshape(b, t, d)


def setup_inputs(seed: int = 0) -> dict:
    key = jax.random.key(seed)
    ks = jax.random.split(key, 24)
    f32 = jnp.float32
    nrm = lambda k, shape, s: jax.random.normal(k, shape, f32) * s
    dt = jnp.exp(jax.random.uniform(ks[9], (N_DN_LAYERS, 2, DN_NV), f32, math.log(1e-3), math.log(1e-1)))
    return {
        'x_prompt': nrm(ks[0], (BATCH, SEQ, D_MODEL), 1.0),
        'x_sample': nrm(ks[1], (DEC_BATCH, DEC_SEQ, D_MODEL), 1.0),
        'state_dn': nrm(ks[2], (DEC_BATCH, N_DN_LAYERS, 2, DN_NV, DN_DK, DN_DV), 0.1),
        'c': nrm(ks[3], (DEC_BATCH, D_MODEL), 1.0),
        'c_ctx': nrm(ks[4], (D_MODEL,), 1.0),
        'ada_w': nrm(ks[5], (DEPTH, D_MODEL, 6 * D_MODEL), 0.5 * D_MODEL ** -0.5),
        'ada_b': nrm(ks[6], (DEPTH, 6 * D_MODEL), 0.02),
        'norm_g': 1.0 + nrm(ks[7], (DEPTH, 2, D_MODEL), 0.05),
        'final_g': 1.0 + nrm(ks[8], (D_MODEL,), 0.05),
        'dn_w_in': nrm(ks[10], (N_DN_LAYERS, D_MODEL, DN_IN), D_MODEL ** -0.5),
        'dn_conv': nrm(ks[11], (N_DN_LAYERS, DN_CONV, 2 * DN_QK + DN_V), DN_CONV ** -0.5),
        'dn_a_log': jnp.log(jax.random.uniform(ks[12], (N_DN_LAYERS, 2, DN_NV), f32, 1.0, 16.0)),
        'dn_dt_bias': dt + jnp.log(-jnp.expm1(-dt)),
        'dn_onorm_g': 1.0 + nrm(ks[13], (N_DN_LAYERS, DN_DV), 0.05),
        'dn_w_out': nrm(ks[14], (N_DN_LAYERS, DN_V, D_MODEL), DN_V ** -0.5),
        'pool_w': nrm(ks[15], (N_POOL_LAYERS, N_POOL_GROUPS, POOL_GROUP, POOL_GROUP), POOL_GROUP ** -0.5),
        'pool_scale': 1.0 + nrm(ks[16], (N_POOL_LAYERS, D_MODEL), 0.1),
        'moe_router': nrm(ks[17], (DEPTH, D_MODEL, N_EXPERTS), D_MODEL ** -0.5),
        'moe_bias': nrm(ks[18], (DEPTH, N_EXPERTS), 0.01),
        'moe_w_gu': nrm(ks[19], (DEPTH, N_EXPERTS, D_MODEL, 2 * D_EXPERT), D_MODEL ** -0.5),
        'moe_w_down': nrm(ks[20], (DEPTH, N_EXPERTS, D_EXPERT, D_MODEL), D_EXPERT ** -0.5),
        'sh_w_gu': nrm(ks[21], (DEPTH, D_MODEL, 2 * D_SHARED), D_MODEL ** -0.5),
        'sh_w_down': nrm(ks[22], (DEPTH, D_SHARED, D_MODEL), D_SHARED ** -0.5),
    }


def reference(x_prompt, x_sample, state_dn, c, c_ctx, ada_w, ada_b, norm_g, final_g,
              dn_w_in, dn_conv, dn_a_log, dn_dt_bias, dn_onorm_g, dn_w_out,
              pool_w, pool_scale, moe_router, moe_bias, moe_w_gu, moe_w_down, sh_w_gu, sh_w_down):
    def trunk(x, cvec, latent, s_cache):
        b = x.shape[0]
        states = []
        for i in range(DEPTH):
            j = i // N_MIXERS
            sh1, sc1, g1, sh2, sc2, g2 = adaln(cvec, ada_w[i], ada_b[i])
            h = rmsnorm(x, norm_g[i, 0]) * (1.0 + sc1) + sh1
            if i % N_MIXERS == 0:
                if latent:
                    s0 = s_cache[:, j]
                else:
                    s0 = jnp.zeros((b, 2, DN_NV, DN_DK, DN_DV), jnp.float32)
                out, s_fin = deltanet_mixer(h, s0, dn_w_in[j], dn_conv[j], dn_a_log[j], dn_dt_bias[j],
                                            dn_onorm_g[j], dn_w_out[j])
                states.append(s_fin)
            else:
                out = pool_mixer(h, latent, pool_w[j], pool_scale[j])
            x = x + g1 * out
            h = rmsnorm(x, norm_g[i, 1]) * (1.0 + sc2) + sh2
            x = x + g2 * moe_ffn(h, moe_router[i], moe_bias[i], moe_w_gu[i], moe_w_down[i],
                                 sh_w_gu[i], sh_w_down[i])
        return rmsnorm(x, final_g), states

    yp, ctx_states = trunk(x_prompt, c_ctx[None, :], False, None)
    ys, _ = trunk(x_sample, c, True, state_dn)
    y_prompt = yp.astype(x_prompt.dtype)
    y_sample = ys.astype(x_sample.dtype)
    new_state_dn = jnp.stack(ctx_states, axis=1).astype(x_prompt.dtype)
    return (y_prompt, y_sample, new_state_dn)
```

```python
import functools
import math

import jax
import jax.numpy as jnp
from jax import lax
from jax.experimental import pallas as pl
from jax.experimental.pallas import tpu as pltpu

F32 = jnp.float32
BF16 = jnp.bfloat16
I32 = jnp.int32
U32 = jnp.uint32

D_MODEL = 2048
BATCH = 16
SEQ = 256
DEC_BATCH = 8
DEC_SEQ = 2048
GRID_W = 64
NP_TOK = BATCH * SEQ
NS_TOK = DEC_BATCH * DEC_SEQ
N_TOK = NP_TOK + NS_TOK
N_MOD = 16

DN_DK = 128
DN_DV = 128
DN_NK = D_MODEL // DN_DK
DN_NV = 2 * DN_NK
DN_QK = DN_NK * DN_DK
DN_V = DN_NV * DN_DV
DN_CONV = 5
CHUNK = 64
N_GATE = 4 * DN_NV

POOL_WINDOWS = (2, 4, 8, 16)
POOL_GROUP = D_MODEL // len(POOL_WINDOWS)

N_EXPERTS = 64
TOP_K = 8
N_EXPERT_GROUPS = 8
GROUP_SIZE = N_EXPERTS // N_EXPERT_GROUPS
TOPK_GROUPS = 4
D_EXPERT = 512
D_SHARED = 512
ROUTED_SCALE = 2.5
EPS = 1e-6

MOE_BLK = 256
N_ASSIGN = N_TOK * TOP_K
N_BLOCKS = (N_ASSIGN + N_EXPERTS * (MOE_BLK - 1) + MOE_BLK - 1) // MOE_BLK
N_SLOTS = N_BLOCKS * MOE_BLK
HALF = D_MODEL // 2
COMB_TM = 128
VMEM_LIMIT = 56 * 1024 * 1024

_NT = (((1,), (1,)), ((), ()))
_TN = (((0,), (0,)), ((), ()))


def _nt(a, b, precision=None):
    return lax.dot_general(a, b, _NT, preferred_element_type=F32, precision=precision)


def _tn(a, b):
    return lax.dot_general(a, b, _TN, preferred_element_type=F32)


def _dot(a, b, precision=None):
    return jnp.dot(a, b, preferred_element_type=F32, precision=precision)


def _silu(x):
    return x * jax.nn.sigmoid(x)


def _mod_row(tile_start):
    return jnp.where(tile_start < NP_TOK, 0, 1 + (tile_start - NP_TOK) // DEC_SEQ)


def _params(sem, vmem=VMEM_LIMIT):
    return pltpu.CompilerParams(dimension_semantics=sem, vmem_limit_bytes=vmem)


def _pack_halves(y):
    a = lax.bitcast_convert_type(y[:, :HALF].astype(BF16).astype(F32), U32)
    b = lax.bitcast_convert_type(y[:, HALF:].astype(BF16).astype(F32), U32)
    return a | (b >> 16)


def _unpack_halves(u):
    a = lax.bitcast_convert_type(u & jnp.uint32(0xFFFF0000), F32)
    b = lax.bitcast_convert_type(u << 16, F32)
    return a, b


def _ada_kernel(c_ref, w_ref, b_ref, o_ref):
    a = _silu(c_ref[...]).astype(BF16)
    o_ref[...] = _dot(a, w_ref[...].astype(BF16)) + b_ref[...]


def _ada_mod(cvec, ada_w, ada_b):
    depth, d, n = ada_w.shape
    tn = 1024
    return pl.pallas_call(
        _ada_kernel,
        grid=(depth, n // tn),
        in_specs=[
            pl.BlockSpec((N_MOD, d), lambda l, j: (0, 0)),
            pl.BlockSpec((None, d, tn), lambda l, j: (l, 0, j)),
            pl.BlockSpec((None, 1, tn), lambda l, j: (l, 0, j)),
        ],
        out_specs=pl.BlockSpec((None, N_MOD, tn), lambda l, j: (l, 0, j)),
        out_shape=jax.ShapeDtypeStruct((depth, N_MOD, n), F32),
        compiler_params=_params(("parallel", "parallel")),
        name="ada_mod",
    )(cvec, ada_w, ada_b.reshape(depth, 1, n))


def _norm_mod(x, g, sc, sh):
    ms = jnp.mean(x * x, axis=-1, keepdims=True)
    return x * lax.rsqrt(ms + EPS) * g * (1.0 + sc) + sh


def _in_proj_kernel(x_ref, g_ref, sc_ref, sh_ref, w_ref, wab_ref, wab_lo_ref, o_ref, ab_ref, h_scr):
    @pl.when(pl.program_id(1) == 0)
    def _():
        hf = _norm_mod(x_ref[...], g_ref[...], sc_ref[...], sh_ref[...])
        h = hf.astype(BF16)
        h_scr[...] = h
        h_lo = (hf - h.astype(F32)).astype(BF16)
        ab_ref[...] = _dot(h, wab_ref[...]) + (_dot(h_lo, wab_ref[...]) + _dot(h, wab_lo_ref[...]))

    o_ref[...] = _dot(h_scr[...], w_ref[...]).astype(o_ref.dtype)


def _in_proj(x, g, sc, sh, w, wab_f32):
    wab = wab_f32.astype(BF16)
    wab_lo = (wab_f32 - wab.astype(F32)).astype(BF16)
    tm, tn = 1024, 512
    n = w.shape[1]
    mod_spec = pl.BlockSpec((None, 1, D_MODEL), lambda i, j: (_mod_row(i * tm), 0, 0))
    return pl.pallas_call(
        _in_proj_kernel,
        grid=(N_TOK // tm, n // tn),
        in_specs=[
            pl.BlockSpec((tm, D_MODEL), lambda i, j: (i, 0)),
            pl.BlockSpec((1, D_MODEL), lambda i, j: (0, 0)),
            mod_spec,
            mod_spec,
            pl.BlockSpec((D_MODEL, tn), lambda i, j: (0, j)),
            pl.BlockSpec((D_MODEL, N_GATE), lambda i, j: (0, 0)),
            pl.BlockSpec((D_MODEL, N_GATE), lambda i, j: (0, 0)),
        ],
        out_specs=[
            pl.BlockSpec((tm, tn), lambda i, j: (i, j)),
            pl.BlockSpec((tm, N_GATE), lambda i, j: (i, 0)),
        ],
        out_shape=[
            jax.ShapeDtypeStruct((N_TOK, n), BF16),
            jax.ShapeDtypeStruct((N_TOK, N_GATE), F32),
        ],
        scratch_shapes=[pltpu.VMEM((tm, D_MODEL), BF16)],
        compiler_params=_params(("parallel", "arbitrary")),
        name="dn_in_proj",
    )(x, g, sc, sh, w, wab, wab_lo)


def _gates_kernel(ab_ref, alog_ref, dt_ref, o_ref):
    tm = ab_ref.shape[0]
    a = ab_ref[...]
    z = a + dt_ref[...]
    softplus = jnp.maximum(z, 0.0) + jnp.log(1.0 + jnp.exp(-jnp.abs(z)))
    g = -jnp.exp(alog_ref[...]) * softplus
    beta = jax.nn.sigmoid(a)
    ii = lax.broadcasted_iota(I32, (tm, tm), 0)
    jj = lax.broadcasted_iota(I32, (tm, tm), 1)
    same = (ii // CHUNK) == (jj // CHUNK)
    m_f = jnp.where(same & (jj <= ii), 1.0, 0.0).astype(F32)
    m_b = jnp.where(same & (jj >= ii), 1.0, 0.0).astype(F32)
    gc_f = _dot(m_f, g, precision=lax.Precision.HIGHEST)
    gc_b = _dot(m_b, g, precision=lax.Precision.HIGHEST)
    lane = lax.broadcasted_iota(I32, a.shape, 1)
    is_g = (lane % (2 * DN_NV)) < DN_NV
    o_ref[...] = jnp.where(is_g, jnp.where(lane < 2 * DN_NV, gc_f, gc_b), beta)


def _gates(ab, a_log, dt_bias):
    tm = 256
    zeros = jnp.zeros((DN_NV,), F32)
    alog = jnp.concatenate([a_log[0], zeros, a_log[1], zeros]).reshape(1, N_GATE)
    dt = jnp.concatenate([dt_bias[0], zeros, dt_bias[1], zeros]).reshape(1, N_GATE)
    return pl.pallas_call(
        _gates_kernel,
        grid=(N_TOK // tm,),
        in_specs=[
            pl.BlockSpec((tm, N_GATE), lambda i: (i, 0)),
            pl.BlockSpec((1, N_GATE), lambda i: (0, 0)),
            pl.BlockSpec((1, N_GATE), lambda i: (0, 0)),
        ],
        out_specs=pl.BlockSpec((tm, N_GATE), lambda i: (i, 0)),
        out_shape=jax.ShapeDtypeStruct((N_TOK, N_GATE), F32),
        compiler_params=_params(("parallel",)),
        name="dn_gates",
    )(ab, alog, dt)


CONV_ROWS = 2048
CONV_TC = 512


def _conv_kernel(x_ref, w_ref, o_ref):
    i = pl.program_id(0)
    j = pl.program_id(1)
    rows = x_ref.shape[0]
    seq_len = jnp.where(i * rows < NP_TOK, SEQ, DEC_SEQ)
    pos = lax.broadcasted_iota(I32, (rows, 1), 0) & (seq_len - 1)
    x = x_ref[...].astype(F32)
    half = DN_CONV // 2
    acc = x * w_ref[half:half + 1, :]
    for s in range(-half, half + 1):
        if s == 0:
            continue
        xs = pltpu.roll(x, (-s) % rows, 0)
        valid = (pos + s >= 0) & (pos + s < seq_len)
        acc = acc + jnp.where(valid, xs, 0.0) * w_ref[half + s:half + s + 1, :]
    y = _silu(acc)

    @pl.when(j >= (2 * DN_QK) // CONV_TC)
    def _():
        o_ref[...] = y.astype(o_ref.dtype)

    @pl.when(j < (2 * DN_QK) // CONV_TC)
    def _():
        qscale = jnp.where(j < DN_QK // CONV_TC, DN_DK ** -0.5, 1.0)
        for h in range(CONV_TC // DN_DK):
            sl = slice(h * DN_DK, (h + 1) * DN_DK)
            yh = y[:, sl]
            inv = lax.rsqrt(jnp.sum(yh * yh, axis=-1, keepdims=True) + EPS) * qscale
            o_ref[:, sl] = (yh * inv).astype(o_ref.dtype)


def _conv_qkv(proj, conv_w):
    nch = 2 * DN_QK + DN_V
    return pl.pallas_call(
        _conv_kernel,
        grid=(N_TOK // CONV_ROWS, nch // CONV_TC),
        in_specs=[
            pl.BlockSpec((CONV_ROWS, CONV_TC), lambda i, j: (i, j)),
            pl.BlockSpec((DN_CONV, CONV_TC), lambda i, j: (0, j)),
        ],
        out_specs=pl.BlockSpec((CONV_ROWS, CONV_TC), lambda i, j: (i, j)),
        out_shape=jax.ShapeDtypeStruct((N_TOK, nch), BF16),
        compiler_params=_params(("parallel", "parallel")),
        name="dn_conv",
    )(proj, conv_w)


def _delta_kernel(*refs, n_chunks, has_s0, write_state):
    q_ref, k_ref, v_ref, gc_ref, rows_ref = refs[:5]
    pos = 5
    if has_s0:
        s0_ref = refs[pos]
        pos += 1
    o_ref = refs[pos]
    pos += 1
    if write_state:
        sfin_ref = refs[pos]
        pos += 1
    s_scr = refs[pos]

    hk = pl.program_id(1)
    if has_s0:
        for d in range(2):
            for r in range(2):
                s_scr[2 * d + r] = s0_ref[d, r]
    else:
        s_scr[...] = jnp.zeros(s_scr.shape, F32)
    o_ref[...] = jnp.zeros(o_ref.shape, F32)

    ii = lax.broadcasted_iota(I32, (CHUNK, CHUNK), 0)
    jj = lax.broadcasted_iota(I32, (CHUNK, CHUNK), 1)
    eye = jnp.where(ii == jj, 1.0, 0.0).astype(F32)
    lane = lax.broadcasted_iota(I32, (CHUNK, N_GATE), 1)
    pair_masks = [((ii >> (l + 1)) == (jj >> (l + 1))) & ((ii >> l) != (jj >> l))
                  for l in range(int(math.log2(CHUNK)))]

    def body(t, carry):
        for d in range(2):
            c = t if d == 0 else n_chunks - 1 - t
            r0 = pl.multiple_of(c * CHUNK, CHUNK)
            k = k_ref[pl.ds(r0, CHUNK), :]
            q = q_ref[pl.ds(r0, CHUNK), :]
            v = v_ref[pl.ds(r0, CHUNK), :]
            gates = gc_ref[pl.ds(r0, CHUNK), :]
            rows = rows_ref[c]
            kf = k.astype(F32)
            qf = q.astype(F32)
            kk = _nt(k, k)
            qk = _nt(q, k)
            incl = (jj <= ii) if d == 0 else (jj >= ii)
            strict = (jj < ii) if d == 0 else (jj > ii)
            last = CHUNK - 1 if d == 0 else 0
            for r in range(2):
                lane_g = d * 2 * DN_NV + 2 * hk + r
                gcc = jnp.sum(jnp.where(lane == lane_g, gates, 0.0), axis=1, keepdims=True)
                bc = jnp.sum(jnp.where(lane == lane_g + DN_NV, gates, 0.0), axis=1, keepdims=True)
                gcr = rows[2 * d + r:2 * d + r + 1, :]
                gl = gcr[:, last:last + 1]
                diff = gcc - gcr
                dec = jnp.where(incl, jnp.exp(jnp.where(incl, diff, 0.0)), 0.0)
                lm = jnp.where(strict, bc * kk * dec, 0.0)
                tm = eye - jnp.where(pair_masks[0], lm, 0.0)
                lb = lm.astype(BF16)
                for lvl in range(1, len(pair_masks)):
                    tb = tm.astype(BF16)
                    bt = _dot(jnp.where(pair_masks[lvl], lb, jnp.zeros_like(lb)), tb)
                    tm = tm - _dot(tb, bt.astype(BF16))
                egc = jnp.exp(gcc)
                vr = v[:, r * DN_DV:(r + 1) * DN_DV].astype(F32)
                rhs = jnp.concatenate([(vr * bc).astype(BF16), (kf * (bc * egc)).astype(BF16)], axis=1)
                uw = _dot(tm.astype(BF16), rhs)
                u = uw[:, :DN_DV]
                w = uw[:, DN_DV:]
                s = s_scr[2 * d + r]
                wq = jnp.concatenate([w.astype(BF16), (qf * egc).astype(BF16)], axis=0)
                ws_qs = _dot(wq, s.astype(BF16))
                v_new = (u - ws_qs[:CHUNK]).astype(BF16)
                attn = (qk * dec).astype(BF16)
                o = ws_qs[CHUNK:] + _dot(attn, v_new)
                kd = (kf * jnp.exp(gl - gcc)).astype(BF16)
                s_scr[2 * d + r] = s * jnp.exp(gl) + _tn(kd, v_new)
                o_ref[pl.ds(r0, CHUNK), r * DN_DV:(r + 1) * DN_DV] += o
        return carry

    lax.fori_loop(0, n_chunks, body, 0)
    if write_state:
        for d in range(2):
            for r in range(2):
                sfin_ref[d, r] = s_scr[2 * d + r]


def _delta(qkv, gc, rows, s0, *, n_seq, seq_len, row_blk0, write_state):
    n_chunks = seq_len // CHUNK
    has_s0 = s0 is not None
    kern = functools.partial(_delta_kernel, n_chunks=n_chunks, has_s0=has_s0, write_state=write_state)
    in_specs = [
        pl.BlockSpec((seq_len, DN_DK), lambda b, h: (row_blk0 + b, h)),
        pl.BlockSpec((seq_len, DN_DK), lambda b, h: (row_blk0 + b, DN_NK + h)),
        pl.BlockSpec((seq_len, 2 * DN_DV), lambda b, h: (row_blk0 + b, DN_NK + h)),
        pl.BlockSpec((seq_len, N_GATE), lambda b, h: (row_blk0 + b, 0)),
        pl.BlockSpec((None, None, n_chunks, 8, CHUNK), lambda b, h: (b, h, 0, 0, 0)),
    ]
    args = [qkv, qkv, qkv, gc, rows]
    state_spec = pl.BlockSpec((None, None, 2, 2, DN_DK, DN_DV), lambda b, h: (b, 0, 0, h, 0, 0))
    if has_s0:
        in_specs.append(state_spec)
        args.append(s0)
    out_specs = [pl.BlockSpec((seq_len, 2 * DN_DV), lambda b, h: (b, h))]
    out_shape = [jax.ShapeDtypeStruct((n_seq * seq_len, DN_V), F32)]
    if write_state:
        out_specs.append(state_spec)
        out_shape.append(jax.ShapeDtypeStruct((n_seq, 1, 2, DN_NV, DN_DK, DN_DV), F32))
    return pl.pallas_call(
        kern,
        grid=(n_seq, DN_NK),
        in_specs=in_specs,
        out_specs=out_specs,
        out_shape=out_shape,
        scratch_shapes=[pltpu.VMEM((4, DN_DK, DN_DV), F32)],
        compiler_params=_params(("parallel", "parallel")),
        name="dn_delta_%d" % seq_len,
    )(*args)


def _gate_rows(gc, n_seq, seq_len):
    n_chunks = seq_len // CHUNK
    g = gc.reshape(n_seq, n_chunks, CHUNK, 2, 2, DN_NK, 2)
    g = g.transpose(0, 5, 1, 4, 3, 6, 2)
    return g.reshape(n_seq, DN_NK, n_chunks, 8, CHUNK)


def _deltanet_core(x, g, sc1, sh1, w_in, conv_w, a_log, dt_bias, state_dn, sample=True):
    nqkv = 2 * DN_QK + 2 * DN_V
    proj, ab = _in_proj(x, g, sc1, sh1, w_in[:, :nqkv].astype(BF16), w_in[:, nqkv:])
    gc = _gates(ab, a_log, dt_bias)
    qkv = _conv_qkv(proj, conv_w)
    o_p, new_state = _delta(qkv, gc, _gate_rows(gc[:NP_TOK], BATCH, SEQ), None,
                            n_seq=BATCH, seq_len=SEQ, row_blk0=0, write_state=True)
    if not sample:
        return o_p, proj, new_state
    (o_s,) = _delta(qkv, gc, _gate_rows(gc[NP_TOK:], DEC_BATCH, DEC_SEQ), state_dn,
                    n_seq=DEC_BATCH, seq_len=DEC_SEQ, row_blk0=NP_TOK // DEC_SEQ, write_state=False)
    return jnp.concatenate([o_p, o_s], axis=0), proj, new_state


def _dn_out_kernel(o_ref, z_ref, og_ref, w_ref, x_ref, g1_ref, out_ref, a_scr):
    @pl.when(pl.program_id(1) == 0)
    def _():
        for h in range(DN_NV):
            sl = slice(h * DN_DV, (h + 1) * DN_DV)
            o = o_ref[:, sl]
            z = z_ref[:, sl].astype(F32)
            on = o * lax.rsqrt(jnp.mean(o * o, axis=-1, keepdims=True) + EPS) * og_ref[...]
            a_scr[:, sl] = (on * _silu(z)).astype(BF16)

    out_ref[...] = x_ref[...] + g1_ref[...] * _dot(a_scr[...], w_ref[...])


def _dn_out(o, proj, onorm_g, w_out, x, g1):
    tm, tn = 256, 1024
    zblk = (2 * DN_QK + DN_V) // DN_V
    return pl.pallas_call(
        _dn_out_kernel,
        grid=(N_TOK // tm, D_MODEL // tn),
        in_specs=[
            pl.BlockSpec((tm, DN_V), lambda i, j: (i, 0)),
            pl.BlockSpec((tm, DN_V), lambda i, j: (i, zblk)),
            pl.BlockSpec((1, DN_DV), lambda i, j: (0, 0)),
            pl.BlockSpec((DN_V, tn), lambda i, j: (0, j)),
            pl.BlockSpec((tm, tn), lambda i, j: (i, j)),
            pl.BlockSpec((None, 1, tn), lambda i, j: (_mod_row(i * tm), 0, j)),
        ],
        out_specs=pl.BlockSpec((tm, tn), lambda i, j: (i, j)),
        out_shape=jax.ShapeDtypeStruct((N_TOK, D_MODEL), F32),
        scratch_shapes=[pltpu.VMEM((tm, DN_V), BF16)],
        compiler_params=_params(("parallel", "arbitrary")),
        name="dn_out",
    )(o, proj, onorm_g, w_out, x, g1)


def _rstd_kernel(x_ref, o_ref):
    x = x_ref[...]
    o_ref[...] = lax.rsqrt(jnp.mean(x * x, axis=-1, keepdims=True) + EPS)


def _rstd(x):
    tm = 1024
    return pl.pallas_call(
        _rstd_kernel,
        grid=(N_TOK // tm,),
        in_specs=[pl.BlockSpec((tm, D_MODEL), lambda i: (i, 0))],
        out_specs=pl.BlockSpec((tm, 1), lambda i: (i, 0)),
        out_shape=jax.ShapeDtypeStruct((N_TOK, 1), F32),
        compiler_params=_params(("parallel",)),
        name="rstd",
    )(x)


POOL_ROWS = 2048


def _window_sum(x, pos, length, stride, w):
    rows = x.shape[0]
    m = w // 2

    def shifted(a, off):
        valid = (pos + off >= 0) & (pos + off < length)
        return jnp.where(valid, pltpu.roll(a, (-off * stride) % rows, 0), 0.0)

    fwd = x
    bwd = x
    step = 1
    while step < m:
        fwd = fwd + shifted(fwd, step)
        bwd = bwd + shifted(bwd, -step)
        step *= 2
    return fwd + shifted(bwd, -1)


def _window_count(pos, length, w):
    lo = jnp.maximum(pos - w // 2, 0)
    hi = jnp.minimum(pos + (w - w // 2) - 1, length - 1)
    return (hi - lo + 1).astype(F32)


def _pool_kernel(x_ref, rstd_ref, g_ref, sc_ref, sh_ref, w_ref, ps_ref, g1_ref, o_ref):
    i = pl.program_id(0)
    grp = pl.program_id(1)
    x = x_ref[...]
    h = x * rstd_ref[...] * g_ref[...] * (1.0 + sc_ref[...]) + sh_ref[...]
    ridx = lax.broadcasted_iota(I32, (POOL_ROWS, 1), 0)

    def finish(pooled):
        p = (pooled - h).astype(BF16)
        y = _dot(p, w_ref[...]) * ps_ref[...]
        o_ref[...] = x + g1_ref[...] * y

    for gi, w in enumerate(POOL_WINDOWS):
        @pl.when((grp == gi) & (i * POOL_ROWS < NP_TOK))
        def _(w=w):
            pos = ridx & (SEQ - 1)
            s = _window_sum(h, pos, SEQ, 1, w)
            finish(s / _window_count(pos, SEQ, w))

        @pl.when((grp == gi) & (i * POOL_ROWS >= NP_TOK))
        def _(w=w):
            col = ridx & (GRID_W - 1)
            row = ridx // GRID_W
            n_rows = POOL_ROWS // GRID_W
            s = _window_sum(h, col, GRID_W, 1, w)
            s = _window_sum(s, row, n_rows, GRID_W, w)
            finish(s / (_window_count(col, GRID_W, w) * _window_count(row, n_rows, w)))


def _pool_mixer(x, rstd, g, sc, sh, pool_w, pool_scale, g1):
    ngrp = len(POOL_WINDOWS)
    mod = lambda i, j: (_mod_row(i * POOL_ROWS), 0, j)
    return pl.pallas_call(
        _pool_kernel,
        grid=(N_TOK // POOL_ROWS, ngrp),
        in_specs=[
            pl.BlockSpec((POOL_ROWS, POOL_GROUP), lambda i, j: (i, j)),
            pl.BlockSpec((POOL_ROWS, 1), lambda i, j: (i, 0)),
            pl.BlockSpec((1, POOL_GROUP), lambda i, j: (0, j)),
            pl.BlockSpec((None, 1, POOL_GROUP), mod),
            pl.BlockSpec((None, 1, POOL_GROUP), mod),
            pl.BlockSpec((None, POOL_GROUP, POOL_GROUP), lambda i, j: (j, 0, 0)),
            pl.BlockSpec((1, POOL_GROUP), lambda i, j: (0, j)),
            pl.BlockSpec((None, 1, POOL_GROUP), mod),
        ],
        out_specs=pl.BlockSpec((POOL_ROWS, POOL_GROUP), lambda i, j: (i, j)),
        out_shape=jax.ShapeDtypeStruct((N_TOK, D_MODEL), F32),
        compiler_params=_params(("parallel", "parallel")),
        name="pool_mixer",
    )(x, rstd, g, sc, sh, pool_w, pool_scale, g1)


def _moe_pre_kernel(x_ref, g_ref, sc_ref, sh_ref, rw_ref, rb_ref, hp_ref, idx_ref, wt_ref):
    tm = x_ref.shape[0]
    h = _norm_mod(x_ref[...], g_ref[...], sc_ref[...], sh_ref[...])
    hp_ref[...] = _pack_halves(h)
    logits = _nt(rw_ref[...], h, precision=lax.Precision.HIGHEST)
    s = jax.nn.sigmoid(logits)
    choice = s + rb_ref[...]
    neg = -jnp.inf
    row8 = lax.broadcasted_iota(I32, (GROUP_SIZE, tm), 0)
    rowg = lax.broadcasted_iota(I32, (N_EXPERT_GROUPS, tm), 0)

    def first_max(vals, riota, n):
        m = jnp.max(vals, axis=0, keepdims=True)
        i = jnp.min(jnp.where(vals == m, riota, n), axis=0, keepdims=True)
        return m, i

    gscore = jnp.zeros((N_EXPERT_GROUPS, tm), F32)
    for g in range(N_EXPERT_GROUPS):
        cg = choice[g * GROUP_SIZE:(g + 1) * GROUP_SIZE, :]
        m1, i1 = first_max(cg, row8, GROUP_SIZE)
        m2 = jnp.max(jnp.where(row8 == i1, neg, cg), axis=0, keepdims=True)
        gscore = jnp.where(rowg == g, m1 + m2, gscore)
    gsel = jnp.zeros((N_EXPERT_GROUPS, tm), F32)
    for _ in range(TOPK_GROUPS):
        _, i = first_max(gscore, rowg, N_EXPERT_GROUPS)
        gsel = jnp.where(rowg == i, 1.0, gsel)
        gscore = jnp.where(rowg == i, neg, gscore)
    masked = jnp.concatenate(
        [jnp.where(gsel[g:g + 1, :] > 0.0, choice[g * GROUP_SIZE:(g + 1) * GROUP_SIZE, :], neg)
         for g in range(N_EXPERT_GROUPS)], axis=0)
    rowe = lax.broadcasted_iota(I32, (N_EXPERTS, tm), 0)
    rowk = lax.broadcasted_iota(I32, (TOP_K, tm), 0)
    idx = jnp.zeros((TOP_K, tm), I32)
    wts = jnp.zeros((TOP_K, tm), F32)
    for kk in range(TOP_K):
        _, i = first_max(masked, rowe, N_EXPERTS)
        sel = rowe == i
        wk = jnp.sum(jnp.where(sel, s, 0.0), axis=0, keepdims=True)
        idx = jnp.where(rowk == kk, i, idx)
        wts = jnp.where(rowk == kk, wk, wts)
        masked = jnp.where(sel, neg, masked)
    wts = wts / jnp.sum(wts, axis=0, keepdims=True) * ROUTED_SCALE
    idx_ref[...] = idx
    wt_ref[...] = wts


def _moe_pre(x, g, sc, sh, router_wt, router_b):
    tm = 256
    mod_spec = pl.BlockSpec((None, 1, D_MODEL), lambda i: (_mod_row(i * tm), 0, 0))
    return pl.pallas_call(
        _moe_pre_kernel,
        grid=(N_TOK // tm,),
        in_specs=[
            pl.BlockSpec((tm, D_MODEL), lambda i: (i, 0)),
            pl.BlockSpec((1, D_MODEL), lambda i: (0, 0)),
            mod_spec,
            mod_spec,
            pl.BlockSpec((N_EXPERTS, D_MODEL), lambda i: (0, 0)),
            pl.BlockSpec((N_EXPERTS, 1), lambda i: (0, 0)),
        ],
        out_specs=[
            pl.BlockSpec((tm, HALF), lambda i: (i, 0)),
            pl.BlockSpec((TOP_K, tm), lambda i: (0, i)),
            pl.BlockSpec((TOP_K, tm), lambda i: (0, i)),
        ],
        out_shape=[
            jax.ShapeDtypeStruct((N_TOK, HALF), U32),
            jax.ShapeDtypeStruct((TOP_K, N_TOK), I32),
            jax.ShapeDtypeStruct((TOP_K, N_TOK), F32),
        ],
        compiler_params=_params(("parallel",)),
        name="moe_pre",
    )(x, g, sc, sh, router_wt, router_b)


def _route(idx_t):
    flat_e = idx_t.T.reshape(-1)
    order = jnp.argsort(flat_e).astype(I32)
    inv_order = jnp.argsort(order).astype(I32)
    counts = jnp.sum(flat_e[:, None] == jnp.arange(N_EXPERTS, dtype=I32)[None, :], axis=0, dtype=I32)
    padded = (counts + MOE_BLK - 1) // MOE_BLK * MOE_BLK
    start = jnp.cumsum(counts) - counts
    pend = jnp.cumsum(padded)
    pstart = pend - padded
    pos = pstart[flat_e] + inv_order - start[flat_e]
    block_start = jnp.arange(N_BLOCKS, dtype=I32) * MOE_BLK
    block_e = jnp.minimum(jnp.sum(pend[None, :] <= block_start[:, None], axis=1), N_EXPERTS - 1).astype(I32)
    slot = jnp.arange(N_SLOTS, dtype=I32)
    slot_e = jnp.repeat(block_e, MOE_BLK)
    within = slot - pstart[slot_e]
    valid = (within < counts[slot_e]) & (slot < pend[N_EXPERTS - 1])
    src = jnp.clip(start[slot_e] + within, 0, N_ASSIGN - 1)
    slot_tok = jnp.where(valid, order[src] // TOP_K, 0).astype(I32)
    n_used = (pend[N_EXPERTS - 1] // MOE_BLK).astype(I32).reshape(1)
    return slot_tok.reshape(N_BLOCKS, MOE_BLK), pos.astype(I32), block_e, n_used


def _expert_kernel(be_ref, nu_ref, tok_hbm, hp_hbm, wgu_ref, wd_ref, y_ref, idx_smem, xbuf, sem_idx, sem_rows):
    b = pl.program_id(0)
    nu = nu_ref[0]

    def idx_copy(blk, slot):
        return pltpu.make_async_copy(tok_hbm.at[blk], idx_smem.at[slot], sem_idx.at[slot])

    def issue_rows(slot):
        def one(r, carry):
            t = idx_smem[slot, r]
            pltpu.make_async_copy(hp_hbm.at[pl.ds(t, 1)], xbuf.at[slot, pl.ds(r, 1)], sem_rows.at[slot]).start()
            return carry
        lax.fori_loop(0, MOE_BLK, one, 0, unroll=8)

    def wait_rows(slot):
        pltpu.make_async_copy(hp_hbm.at[pl.ds(0, MOE_BLK)], xbuf.at[slot], sem_rows.at[slot]).wait()

    @pl.when(b == 0)
    def _():
        idx_copy(0, 0).start()
        idx_copy(0, 0).wait()
        issue_rows(0)

        @pl.when(nu > 1)
        def _():
            idx_copy(1, 1).start()

    nxt = (b + 1) % 2
    cur = b % 2

    @pl.when(b + 1 < nu)
    def _():
        idx_copy(b + 1, nxt).wait()
        issue_rows(nxt)

        @pl.when(b + 2 < nu)
        def _():
            idx_copy(b + 2, cur).start()

    @pl.when(b < nu)
    def _():
        wait_rows(cur)
        xa, xb = _unpack_halves(xbuf[cur])
        gu = _dot(xa.astype(BF16), wgu_ref[:HALF, :]) + _dot(xb.astype(BF16), wgu_ref[HALF:, :])
        act = (_silu(gu[:, :D_EXPERT]) * gu[:, D_EXPERT:]).astype(BF16)
        y_ref[...] = _pack_halves(_dot(act, wd_ref[...]))

    @pl.when(b >= nu)
    def _():
        y_ref[...] = jnp.zeros(y_ref.shape, U32)


def _experts(slot_tok, block_e, n_used, hp, w_gu, w_down):
    grid_spec = pltpu.PrefetchScalarGridSpec(
        num_scalar_prefetch=2,
        grid=(N_BLOCKS,),
        in_specs=[
            pl.BlockSpec(memory_space=pl.ANY),
            pl.BlockSpec(memory_space=pl.ANY),
            pl.BlockSpec((None, D_MODEL, 2 * D_EXPERT), lambda b, be, nu: (be[b], 0, 0)),
            pl.BlockSpec((None, D_EXPERT, D_MODEL), lambda b, be, nu: (be[b], 0, 0)),
        ],
        out_specs=pl.BlockSpec((MOE_BLK, HALF), lambda b, be, nu: (b, 0)),
        scratch_shapes=[
            pltpu.SMEM((2, MOE_BLK), I32),
            pltpu.VMEM((2, MOE_BLK, HALF), U32),
            pltpu.SemaphoreType.DMA((2,)),
            pltpu.SemaphoreType.DMA((2,)),
        ],
    )
    return pl.pallas_call(
        _expert_kernel,
        grid_spec=grid_spec,
        out_shape=jax.ShapeDtypeStruct((N_SLOTS, HALF), U32),
        compiler_params=_params(("arbitrary",)),
        name="moe_experts",
    )(block_e, n_used, slot_tok, hp, w_gu, w_down)


COMB_ROWS = COMB_TM * TOP_K


def _combine_kernel(pos_hbm, y_hbm, x_ref, hp_ref, wt_ref, g2_ref, sgu_ref, sd_ref, fg_ref, o_ref,
                    idx_smem, ybuf, sem_idx, sem_rows, *, final_norm):
    i = pl.program_id(0)
    n = pl.num_programs(0)

    def idx_copy(blk, slot):
        return pltpu.make_async_copy(pos_hbm.at[blk], idx_smem.at[slot], sem_idx.at[slot])

    def issue_rows(slot):
        def one(r, carry):
            p = idx_smem[slot, r]
            pltpu.make_async_copy(y_hbm.at[pl.ds(p, 1)], ybuf.at[slot, pl.ds(r, 1)], sem_rows.at[slot]).start()
            return carry
        lax.fori_loop(0, COMB_ROWS, one, 0, unroll=8)

    def wait_rows(slot):
        pltpu.make_async_copy(y_hbm.at[pl.ds(0, COMB_ROWS)], ybuf.at[slot], sem_rows.at[slot]).wait()

    @pl.when(i == 0)
    def _():
        idx_copy(0, 0).start()
        idx_copy(0, 0).wait()
        issue_rows(0)

        @pl.when(n > 1)
        def _():
            idx_copy(1, 1).start()

    nxt = (i + 1) % 2
    cur = i % 2

    @pl.when(i + 1 < n)
    def _():
        idx_copy(i + 1, nxt).wait()
        issue_rows(nxt)

        @pl.when(i + 2 < n)
        def _():
            idx_copy(i + 2, cur).start()

    ha, hb = _unpack_halves(hp_ref[...])
    gu = _dot(ha.astype(BF16), sgu_ref[:HALF, :]) + _dot(hb.astype(BF16), sgu_ref[HALF:, :])
    act = (_silu(gu[:, :D_SHARED]) * gu[:, D_SHARED:]).astype(BF16)
    ffn = _dot(act, sd_ref[...])
    wait_rows(cur)
    acc_a = ffn[:, :HALF]
    acc_b = ffn[:, HALF:]
    for k in range(TOP_K):
        ya, yb = _unpack_halves(ybuf[cur, k * COMB_TM:(k + 1) * COMB_TM, :])
        wk = wt_ref[:, k:k + 1]
        acc_a = acc_a + wk * ya
        acc_b = acc_b + wk * yb
    xa = x_ref[:, :HALF] + g2_ref[:, :HALF] * acc_a
    xb = x_ref[:, HALF:] + g2_ref[:, HALF:] * acc_b
    if final_norm:
        ms = (jnp.sum(xa * xa, axis=-1, keepdims=True) + jnp.sum(xb * xb, axis=-1, keepdims=True)) / D_MODEL
        inv = lax.rsqrt(ms + EPS)
        xa = xa * inv * fg_ref[:, :HALF]
        xb = xb * inv * fg_ref[:, HALF:]
    o_ref[:, :HALF] = xa
    o_ref[:, HALF:] = xb


def _combine(pos, y, x, hp, wts, g2, sh_gu, sh_down, final_g, *, final_norm):
    n_tiles = N_TOK // COMB_TM
    pos_t = pos.reshape(n_tiles, COMB_TM, TOP_K).transpose(0, 2, 1).reshape(n_tiles, COMB_ROWS)
    kern = functools.partial(_combine_kernel, final_norm=final_norm)
    return pl.pallas_call(
        kern,
        grid=(n_tiles,),
        in_specs=[
            pl.BlockSpec(memory_space=pl.ANY),
            pl.BlockSpec(memory_space=pl.ANY),
            pl.BlockSpec((COMB_TM, D_MODEL), lambda i: (i, 0)),
            pl.BlockSpec((COMB_TM, HALF), lambda i: (i, 0)),
            pl.BlockSpec((COMB_TM, TOP_K), lambda i: (i, 0)),
            pl.BlockSpec((None, 1, D_MODEL), lambda i: (_mod_row(i * COMB_TM), 0, 0)),
            pl.BlockSpec((D_MODEL, 2 * D_SHARED), lambda i: (0, 0)),
            pl.BlockSpec((D_SHARED, D_MODEL), lambda i: (0, 0)),
            pl.BlockSpec((1, D_MODEL), lambda i: (0, 0)),
        ],
        out_specs=pl.BlockSpec((COMB_TM, D_MODEL), lambda i: (i, 0)),
        out_shape=jax.ShapeDtypeStruct((N_TOK, D_MODEL), F32),
        scratch_shapes=[
            pltpu.SMEM((2, COMB_ROWS), I32),
            pltpu.VMEM((2, COMB_ROWS, HALF), U32),
            pltpu.SemaphoreType.DMA((2,)),
            pltpu.SemaphoreType.DMA((2,)),
        ],
        compiler_params=_params(("arbitrary",)),
        name="moe_combine",
    )(pos_t, y, x, hp, wts, g2, sh_gu, sh_down, final_g)


def _moe(x, g, sc, sh, g2, router_w, router_b, w_gu, w_down, sh_gu, sh_down, final_g, *, final_norm):
    hp, idx_t, wt_t = _moe_pre(x, g, sc, sh, router_w.T, router_b.reshape(N_EXPERTS, 1))
    slot_tok, pos, block_e, n_used = _route(idx_t)
    y = _experts(slot_tok, block_e, n_used, hp, w_gu.astype(BF16), w_down.astype(BF16))
    return _combine(pos, y, x, hp, wt_t.T, g2, sh_gu.astype(BF16), sh_down.astype(BF16),
                    final_g, final_norm=final_norm)


def kernel(x_prompt, x_sample, state_dn, c, c_ctx, ada_w, ada_b, norm_g, final_g, dn_w_in, dn_conv, dn_a_log,
           dn_dt_bias, dn_onorm_g, dn_w_out, pool_w, pool_scale, moe_router, moe_bias, moe_w_gu, moe_w_down,
           sh_w_gu, sh_w_down):
    x = jnp.concatenate([x_prompt.reshape(NP_TOK, D_MODEL), x_sample.reshape(NS_TOK, D_MODEL)], axis=0)
    cvec = jnp.concatenate([c_ctx[None, :], c, jnp.zeros((N_MOD - 1 - DEC_BATCH, D_MODEL), F32)], axis=0)
    mod = _ada_mod(cvec, ada_w, ada_b)
    mod = mod.reshape(mod.shape[0], N_MOD, 6, 1, D_MODEL)
    fg = final_g.reshape(1, D_MODEL)

    def mods(layer):
        return [mod[layer, :, p] for p in range(6)]

    sh1, sc1, g1, sh2, sc2, g2 = mods(0)
    o, proj, new_state = _deltanet_core(x, norm_g[0, 0].reshape(1, D_MODEL), sc1, sh1, dn_w_in[0], dn_conv[0],
                                        dn_a_log[0], dn_dt_bias[0], state_dn)
    x = _dn_out(o, proj, dn_onorm_g[0].reshape(1, DN_DV), dn_w_out[0].astype(BF16), x, g1)
    x = _moe(x, norm_g[0, 1].reshape(1, D_MODEL), sc2, sh2, g2, moe_router[0], moe_bias[0], moe_w_gu[0],
             moe_w_down[0], sh_w_gu[0], sh_w_down[0], fg, final_norm=False)

    sh1, sc1, g1, sh2, sc2, g2 = mods(1)
    x = _pool_mixer(x, _rstd(x), norm_g[1, 0].reshape(1, D_MODEL), sc1, sh1, pool_w[0].astype(BF16),
                    pool_scale[0].reshape(1, D_MODEL), g1)
    y = _moe(x, norm_g[1, 1].reshape(1, D_MODEL), sc2, sh2, g2, moe_router[1], moe_bias[1], moe_w_gu[1],
             moe_w_down[1], sh_w_gu[1], sh_w_down[1], fg, final_norm=True)

    y_prompt = y[:NP_TOK].reshape(BATCH, SEQ, D_MODEL)
    y_sample = y[NP_TOK:].reshape(DEC_BATCH, DEC_SEQ, D_MODEL)
    return (y_prompt, y_sample, new_state)
```

```python
import functools
import math

import jax
import jax.numpy as jnp
from jax import lax
from jax.experimental import pallas as pl
from jax.experimental.pallas import tpu as pltpu

F32 = jnp.float32
BF16 = jnp.bfloat16
I32 = jnp.int32
U32 = jnp.uint32

D_MODEL = 2048
BATCH = 16
SEQ = 256
DEC_BATCH = 8
DEC_SEQ = 2048
GRID_W = 64
NP_TOK = BATCH * SEQ
NS_TOK = DEC_BATCH * DEC_SEQ
N_TOK = NP_TOK + NS_TOK
N_MOD = 16

DN_DK = 128
DN_DV = 128
DN_NK = D_MODEL // DN_DK
DN_NV = 2 * DN_NK
DN_QK = DN_NK * DN_DK
DN_V = DN_NV * DN_DV
DN_CONV = 5
CHUNK = 64
DN_HKB = 4
N_GATE = 4 * DN_NV

POOL_WINDOWS = (2, 4, 8, 16)
POOL_GROUP = D_MODEL // len(POOL_WINDOWS)

N_EXPERTS = 64
TOP_K = 8
N_EXPERT_GROUPS = 8
GROUP_SIZE = N_EXPERTS // N_EXPERT_GROUPS
TOPK_GROUPS = 4
D_EXPERT = 512
D_SHARED = 512
ROUTED_SCALE = 2.5
EPS = 1e-6

MOE_BLK = 256
N_ASSIGN = N_TOK * TOP_K
N_BLOCKS = (N_ASSIGN + N_EXPERTS * (MOE_BLK - 1) + MOE_BLK - 1) // MOE_BLK
N_SLOTS = N_BLOCKS * MOE_BLK
HALF = D_MODEL // 2
COMB_TM = 128
VMEM_LIMIT = 56 * 1024 * 1024

_NT = (((1,), (1,)), ((), ()))
_TN = (((0,), (0,)), ((), ()))


def _nt(a, b, precision=None):
    return lax.dot_general(a, b, _NT, preferred_element_type=F32, precision=precision)


def _tn(a, b):
    return lax.dot_general(a, b, _TN, preferred_element_type=F32)


def _dot(a, b, precision=None):
    return jnp.dot(a, b, preferred_element_type=F32, precision=precision)


def _silu(x):
    return x * jax.nn.sigmoid(x)


def _mod_row(tile_start):
    return jnp.where(tile_start < NP_TOK, 0, 1 + (tile_start - NP_TOK) // DEC_SEQ)


def _params(sem, vmem=VMEM_LIMIT):
    return pltpu.CompilerParams(dimension_semantics=sem, vmem_limit_bytes=vmem)


def _pack_halves(y):
    a = lax.bitcast_convert_type(y[:, :HALF].astype(BF16).astype(F32), U32)
    b = lax.bitcast_convert_type(y[:, HALF:].astype(BF16).astype(F32), U32)
    return a | (b >> 16)


def _unpack_halves(u):
    a = lax.bitcast_convert_type(u & jnp.uint32(0xFFFF0000), F32)
    b = lax.bitcast_convert_type(u << 16, F32)
    return a, b


def _ada_kernel(c_ref, w_ref, b_ref, o_ref):
    a = _silu(c_ref[...]).astype(BF16)
    o_ref[...] = _dot(a, w_ref[...].astype(BF16)) + b_ref[...]


def _ada_mod(cvec, ada_w, ada_b):
    depth, d, n = ada_w.shape
    tn = 1024
    return pl.pallas_call(
        _ada_kernel,
        grid=(depth, n // tn),
        in_specs=[
            pl.BlockSpec((N_MOD, d), lambda l, j: (0, 0)),
            pl.BlockSpec((None, d, tn), lambda l, j: (l, 0, j)),
            pl.BlockSpec((None, 1, tn), lambda l, j: (l, 0, j)),
        ],
        out_specs=pl.BlockSpec((None, N_MOD, tn), lambda l, j: (l, 0, j)),
        out_shape=jax.ShapeDtypeStruct((depth, N_MOD, n), F32),
        compiler_params=_params(("parallel", "parallel")),
        name="ada_mod",
    )(cvec, ada_w, ada_b.reshape(depth, 1, n))


def _norm_mod(x, g, sc, sh):
    ms = jnp.mean(x * x, axis=-1, keepdims=True)
    return x * lax.rsqrt(ms + EPS) * g * (1.0 + sc) + sh


def _in_proj_kernel(x_ref, g_ref, sc_ref, sh_ref, w_ref, wab_ref, wab_lo_ref, o_ref, ab_ref, h_scr):
    @pl.when(pl.program_id(1) == 0)
    def _():
        hf = _norm_mod(x_ref[...], g_ref[...], sc_ref[...], sh_ref[...])
        h = hf.astype(BF16)
        h_scr[...] = h
        h_lo = (hf - h.astype(F32)).astype(BF16)
        ab_ref[...] = _dot(h, wab_ref[...]) + (_dot(h_lo, wab_ref[...]) + _dot(h, wab_lo_ref[...]))

    o_ref[...] = _dot(h_scr[...], w_ref[...]).astype(o_ref.dtype)


def _in_proj(x, g, sc, sh, w, wab_f32):
    wab = wab_f32.astype(BF16)
    wab_lo = (wab_f32 - wab.astype(F32)).astype(BF16)
    tm, tn = 1024, 512
    n = w.shape[1]
    mod_spec = pl.BlockSpec((None, 1, D_MODEL), lambda i, j: (_mod_row(i * tm), 0, 0))
    return pl.pallas_call(
        _in_proj_kernel,
        grid=(N_TOK // tm, n // tn),
        in_specs=[
            pl.BlockSpec((tm, D_MODEL), lambda i, j: (i, 0)),
            pl.BlockSpec((1, D_MODEL), lambda i, j: (0, 0)),
            mod_spec,
            mod_spec,
            pl.BlockSpec((D_MODEL, tn), lambda i, j: (0, j)),
            pl.BlockSpec((D_MODEL, N_GATE), lambda i, j: (0, 0)),
            pl.BlockSpec((D_MODEL, N_GATE), lambda i, j: (0, 0)),
        ],
        out_specs=[
            pl.BlockSpec((tm, tn), lambda i, j: (i, j)),
            pl.BlockSpec((tm, N_GATE), lambda i, j: (i, 0)),
        ],
        out_shape=[
            jax.ShapeDtypeStruct((N_TOK, n), BF16),
            jax.ShapeDtypeStruct((N_TOK, N_GATE), F32),
        ],
        scratch_shapes=[pltpu.VMEM((tm, D_MODEL), BF16)],
        compiler_params=_params(("parallel", "arbitrary")),
        name="dn_in_proj",
    )(x, g, sc, sh, w, wab, wab_lo)


def _gates_kernel(ab_ref, alog_ref, dt_ref, o_ref):
    tm = ab_ref.shape[0]
    a = ab_ref[...]
    z = a + dt_ref[...]
    softplus = jnp.maximum(z, 0.0) + jnp.log(1.0 + jnp.exp(-jnp.abs(z)))
    g = -jnp.exp(alog_ref[...]) * softplus
    beta = jax.nn.sigmoid(a)
    ii = lax.broadcasted_iota(I32, (tm, tm), 0)
    jj = lax.broadcasted_iota(I32, (tm, tm), 1)
    same = (ii // CHUNK) == (jj // CHUNK)
    m_f = jnp.where(same & (jj <= ii), 1.0, 0.0).astype(F32)
    m_b = jnp.where(same & (jj >= ii), 1.0, 0.0).astype(F32)
    gc_f = _dot(m_f, g, precision=lax.Precision.HIGHEST)
    gc_b = _dot(m_b, g, precision=lax.Precision.HIGHEST)
    lane = lax.broadcasted_iota(I32, a.shape, 1)
    is_g = (lane % (2 * DN_NV)) < DN_NV
    o_ref[...] = jnp.where(is_g, jnp.where(lane < 2 * DN_NV, gc_f, gc_b), beta)


def _gates(ab, a_log, dt_bias):
    tm = 256
    zeros = jnp.zeros((DN_NV,), F32)
    alog = jnp.concatenate([a_log[0], zeros, a_log[1], zeros]).reshape(1, N_GATE)
    dt = jnp.concatenate([dt_bias[0], zeros, dt_bias[1], zeros]).reshape(1, N_GATE)
    return pl.pallas_call(
        _gates_kernel,
        grid=(N_TOK // tm,),
        in_specs=[
            pl.BlockSpec((tm, N_GATE), lambda i: (i, 0)),
            pl.BlockSpec((1, N_GATE), lambda i: (0, 0)),
            pl.BlockSpec((1, N_GATE), lambda i: (0, 0)),
        ],
        out_specs=pl.BlockSpec((tm, N_GATE), lambda i: (i, 0)),
        out_shape=jax.ShapeDtypeStruct((N_TOK, N_GATE), F32),
        compiler_params=_params(("parallel",)),
        name="dn_gates",
    )(ab, alog, dt)


CONV_ROWS = 2048
CONV_TC = 512


def _conv_kernel(x_ref, w_ref, o_ref):
    i = pl.program_id(0)
    j = pl.program_id(1)
    rows = x_ref.shape[0]
    seq_len = jnp.where(i * rows < NP_TOK, SEQ, DEC_SEQ)
    pos = lax.broadcasted_iota(I32, (rows, 1), 0) & (seq_len - 1)
    x = x_ref[...].astype(F32)
    half = DN_CONV // 2
    acc = x * w_ref[half:half + 1, :]
    for s in range(-half, half + 1):
        if s == 0:
            continue
        xs = pltpu.roll(x, (-s) % rows, 0)
        valid = (pos + s >= 0) & (pos + s < seq_len)
        acc = acc + jnp.where(valid, xs, 0.0) * w_ref[half + s:half + s + 1, :]
    y = _silu(acc)

    @pl.when(j >= (2 * DN_QK) // CONV_TC)
    def _():
        o_ref[...] = y.astype(o_ref.dtype)

    @pl.when(j < (2 * DN_QK) // CONV_TC)
    def _():
        qscale = jnp.where(j < DN_QK // CONV_TC, DN_DK ** -0.5, 1.0)
        for h in range(CONV_TC // DN_DK):
            sl = slice(h * DN_DK, (h + 1) * DN_DK)
            yh = y[:, sl]
            inv = lax.rsqrt(jnp.sum(yh * yh, axis=-1, keepdims=True) + EPS) * qscale
            o_ref[:, sl] = (yh * inv).astype(o_ref.dtype)


def _conv_qkv(proj, conv_w):
    nch = 2 * DN_QK + DN_V
    return pl.pallas_call(
        _conv_kernel,
        grid=(N_TOK // CONV_ROWS, nch // CONV_TC),
        in_specs=[
            pl.BlockSpec((CONV_ROWS, CONV_TC), lambda i, j: (i, j)),
            pl.BlockSpec((DN_CONV, CONV_TC), lambda i, j: (0, j)),
        ],
        out_specs=pl.BlockSpec((CONV_ROWS, CONV_TC), lambda i, j: (i, j)),
        out_shape=jax.ShapeDtypeStruct((N_TOK, nch), BF16),
        compiler_params=_params(("parallel", "parallel")),
        name="dn_conv",
    )(proj, conv_w)


def _delta_kernel(*refs, n_chunks, has_s0, write_state):
    q_ref, k_ref, v_ref, gc_ref, rows_ref = refs[:5]
    pos = 5
    if has_s0:
        s0_ref = refs[pos]
        pos += 1
    o_ref = refs[pos]
    pos += 1
    if write_state:
        sfin_ref = refs[pos]
        pos += 1
    s_scr = refs[pos]

    hk0 = pl.program_id(1) * DN_HKB
    if has_s0:
        for hl in range(DN_HKB):
            for d in range(2):
                for r in range(2):
                    s_scr[4 * hl + 2 * d + r] = s0_ref[d, 2 * hl + r]
    else:
        s_scr[...] = jnp.zeros(s_scr.shape, F32)
    o_ref[...] = jnp.zeros(o_ref.shape, F32)

    ii = lax.broadcasted_iota(I32, (CHUNK, CHUNK), 0)
    jj = lax.broadcasted_iota(I32, (CHUNK, CHUNK), 1)
    eye = jnp.where(ii == jj, 1.0, 0.0).astype(F32)
    lane = lax.broadcasted_iota(I32, (CHUNK, N_GATE), 1)
    pair_masks = [((ii >> (l + 1)) == (jj >> (l + 1))) & ((ii >> l) != (jj >> l))
                  for l in range(int(math.log2(CHUNK)))]

    def body(t, carry):
        chains = []
        for hl in range(DN_HKB):
            for d in range(2):
                c = t if d == 0 else n_chunks - 1 - t
                r0 = pl.multiple_of(c * CHUNK, CHUNK)
                k = k_ref[pl.ds(r0, CHUNK), hl * DN_DK:(hl + 1) * DN_DK]
                q = q_ref[pl.ds(r0, CHUNK), hl * DN_DK:(hl + 1) * DN_DK]
                gates = gc_ref[pl.ds(r0, CHUNK), :]
                rows = rows_ref[hl, c]
                kk = _nt(k, k)
                qk = _nt(q, k)
                incl = (jj <= ii) if d == 0 else (jj >= ii)
                strict = (jj < ii) if d == 0 else (jj > ii)
                last = CHUNK - 1 if d == 0 else 0
                for r in range(2):
                    lane_g = d * 2 * DN_NV + 2 * (hk0 + hl) + r
                    gcc = jnp.sum(jnp.where(lane == lane_g, gates, 0.0), axis=1, keepdims=True)
                    bc = jnp.sum(jnp.where(lane == lane_g + DN_NV, gates, 0.0), axis=1, keepdims=True)
                    gcr = rows[2 * d + r:2 * d + r + 1, :]
                    gl = gcr[:, last:last + 1]
                    dec = jnp.where(incl, jnp.exp(jnp.where(incl, gcc - gcr, 0.0)), 0.0)
                    lm = jnp.where(strict, bc * kk * dec, 0.0)
                    v = v_ref[pl.ds(r0, CHUNK), (2 * hl + r) * DN_DV:(2 * hl + r + 1) * DN_DV]
                    chains.append(dict(k=k, q=q, v=v, r0=r0, si=4 * hl + 2 * d + r, oc=2 * hl + r,
                                       gcc=gcc, bc=bc, gl=gl, attn=(qk * dec).astype(BF16),
                                       lb=lm.astype(BF16), tm=eye - jnp.where(pair_masks[0], lm, 0.0)))
        for lvl in range(1, len(pair_masks)):
            for ch in chains:
                ch["tb"] = ch["tm"].astype(BF16)
                ch["bt"] = _dot(jnp.where(pair_masks[lvl], ch["lb"], jnp.zeros_like(ch["lb"])), ch["tb"])
            for ch in chains:
                ch["tm"] = ch["tm"] - _dot(ch["tb"], ch["bt"].astype(BF16))
        for ch in chains:
            kf = ch["k"].astype(F32)
            egc = jnp.exp(ch["gcc"])
            rhs = jnp.concatenate([(ch["v"].astype(F32) * ch["bc"]).astype(BF16),
                                   (kf * (ch["bc"] * egc)).astype(BF16)], axis=1)
            ch["uw"] = _dot(ch["tm"].astype(BF16), rhs)
            ch["qg"] = (ch["q"].astype(F32) * egc).astype(BF16)
            ch["kd"] = (kf * jnp.exp(ch["gl"] - ch["gcc"])).astype(BF16)
        for ch in chains:
            ch["s"] = s_scr[ch["si"]]
            wq = jnp.concatenate([ch["uw"][:, DN_DV:].astype(BF16), ch["qg"]], axis=0)
            ch["ws_qs"] = _dot(wq, ch["s"].astype(BF16))
        for ch in chains:
            v_new = (ch["uw"][:, :DN_DV] - ch["ws_qs"][:CHUNK]).astype(BF16)
            o = ch["ws_qs"][CHUNK:] + _dot(ch["attn"], v_new)
            s_scr[ch["si"]] = ch["s"] * jnp.exp(ch["gl"]) + _tn(ch["kd"], v_new)
            o_ref[pl.ds(ch["r0"], CHUNK), ch["oc"] * DN_DV:(ch["oc"] + 1) * DN_DV] += o
        return carry

    lax.fori_loop(0, n_chunks, body, 0)
    if write_state:
        for hl in range(DN_HKB):
            for d in range(2):
                for r in range(2):
                    sfin_ref[d, 2 * hl + r] = s_scr[4 * hl + 2 * d + r]


def _delta(qkv, gc, rows, s0, *, n_seq, seq_len, row_blk0, write_state):
    n_chunks = seq_len // CHUNK
    has_s0 = s0 is not None
    kern = functools.partial(_delta_kernel, n_chunks=n_chunks, has_s0=has_s0, write_state=write_state)
    ngrp = DN_NK // DN_HKB
    qk_w = DN_HKB * DN_DK
    v_w = 2 * DN_HKB * DN_DV
    in_specs = [
        pl.BlockSpec((seq_len, qk_w), lambda b, h: (row_blk0 + b, h)),
        pl.BlockSpec((seq_len, qk_w), lambda b, h: (row_blk0 + b, ngrp + h)),
        pl.BlockSpec((seq_len, v_w), lambda b, h: (row_blk0 + b, ngrp + h)),
        pl.BlockSpec((seq_len, N_GATE), lambda b, h: (row_blk0 + b, 0)),
        pl.BlockSpec((None, DN_HKB, n_chunks, 8, CHUNK), lambda b, h: (b, h, 0, 0, 0)),
    ]
    args = [qkv, qkv, qkv, gc, rows]
    state_spec = pl.BlockSpec((None, None, 2, 2 * DN_HKB, DN_DK, DN_DV), lambda b, h: (b, 0, 0, h, 0, 0))
    if has_s0:
        in_specs.append(state_spec)
        args.append(s0)
    out_specs = [pl.BlockSpec((seq_len, v_w), lambda b, h: (b, h))]
    out_shape = [jax.ShapeDtypeStruct((n_seq * seq_len, DN_V), F32)]
    if write_state:
        out_specs.append(state_spec)
        out_shape.append(jax.ShapeDtypeStruct((n_seq, 1, 2, DN_NV, DN_DK, DN_DV), F32))
    return pl.pallas_call(
        kern,
        grid=(n_seq, ngrp),
        in_specs=in_specs,
        out_specs=out_specs,
        out_shape=out_shape,
        scratch_shapes=[pltpu.VMEM((4 * DN_HKB, DN_DK, DN_DV), F32)],
        compiler_params=_params(("parallel", "parallel")),
        name="dn_delta_%d" % seq_len,
    )(*args)


def _gate_rows(gc, n_seq, seq_len):
    n_chunks = seq_len // CHUNK
    g = gc.reshape(n_seq, n_chunks, CHUNK, 2, 2, DN_NK, 2)
    g = g.transpose(0, 5, 1, 4, 3, 6, 2)
    return g.reshape(n_seq, DN_NK, n_chunks, 8, CHUNK)


def _deltanet_core(x, g, sc1, sh1, w_in, conv_w, a_log, dt_bias, state_dn, sample=True):
    nqkv = 2 * DN_QK + 2 * DN_V
    proj, ab = _in_proj(x, g, sc1, sh1, w_in[:, :nqkv].astype(BF16), w_in[:, nqkv:])
    gc = _gates(ab, a_log, dt_bias)
    qkv = _conv_qkv(proj, conv_w)
    o_p, new_state = _delta(qkv, gc, _gate_rows(gc[:NP_TOK], BATCH, SEQ), None,
                            n_seq=BATCH, seq_len=SEQ, row_blk0=0, write_state=True)
    if not sample:
        return o_p, proj, new_state
    (o_s,) = _delta(qkv, gc, _gate_rows(gc[NP_TOK:], DEC_BATCH, DEC_SEQ), state_dn,
                    n_seq=DEC_BATCH, seq_len=DEC_SEQ, row_blk0=NP_TOK // DEC_SEQ, write_state=False)
    return jnp.concatenate([o_p, o_s], axis=0), proj, new_state


def _dn_out_kernel(o_ref, z_ref, og_ref, w_ref, x_ref, g1_ref, out_ref, a_scr):
    @pl.when(pl.program_id(1) == 0)
    def _():
        for h in range(DN_NV):
            sl = slice(h * DN_DV, (h + 1) * DN_DV)
            o = o_ref[:, sl]
            z = z_ref[:, sl].astype(F32)
            on = o * lax.rsqrt(jnp.mean(o * o, axis=-1, keepdims=True) + EPS) * og_ref[...]
            a_scr[:, sl] = (on * _silu(z)).astype(BF16)

    out_ref[...] = x_ref[...] + g1_ref[...] * _dot(a_scr[...], w_ref[...])


def _dn_out(o, proj, onorm_g, w_out, x, g1):
    tm, tn = 256, 1024
    zblk = (2 * DN_QK + DN_V) // DN_V
    return pl.pallas_call(
        _dn_out_kernel,
        grid=(N_TOK // tm, D_MODEL // tn),
        in_specs=[
            pl.BlockSpec((tm, DN_V), lambda i, j: (i, 0)),
            pl.BlockSpec((tm, DN_V), lambda i, j: (i, zblk)),
            pl.BlockSpec((1, DN_DV), lambda i, j: (0, 0)),
            pl.BlockSpec((DN_V, tn), lambda i, j: (0, j)),
            pl.BlockSpec((tm, tn), lambda i, j: (i, j)),
            pl.BlockSpec((None, 1, tn), lambda i, j: (_mod_row(i * tm), 0, j)),
        ],
        out_specs=pl.BlockSpec((tm, tn), lambda i, j: (i, j)),
        out_shape=jax.ShapeDtypeStruct((N_TOK, D_MODEL), F32),
        scratch_shapes=[pltpu.VMEM((tm, DN_V), BF16)],
        compiler_params=_params(("parallel", "arbitrary")),
        name="dn_out",
    )(o, proj, onorm_g, w_out, x, g1)


def _rstd_kernel(x_ref, o_ref):
    x = x_ref[...]
    o_ref[...] = lax.rsqrt(jnp.mean(x * x, axis=-1, keepdims=True) + EPS)


def _rstd(x):
    tm = 1024
    return pl.pallas_call(
        _rstd_kernel,
        grid=(N_TOK // tm,),
        in_specs=[pl.BlockSpec((tm, D_MODEL), lambda i: (i, 0))],
        out_specs=pl.BlockSpec((tm, 1), lambda i: (i, 0)),
        out_shape=jax.ShapeDtypeStruct((N_TOK, 1), F32),
        compiler_params=_params(("parallel",)),
        name="rstd",
    )(x)


POOL_ROWS = 2048


def _window_sum(x, pos, length, stride, w):
    rows = x.shape[0]
    m = w // 2

    def shifted(a, off):
        valid = (pos + off >= 0) & (pos + off < length)
        return jnp.where(valid, pltpu.roll(a, (-off * stride) % rows, 0), 0.0)

    fwd = x
    bwd = x
    step = 1
    while step < m:
        fwd = fwd + shifted(fwd, step)
        bwd = bwd + shifted(bwd, -step)
        step *= 2
    return fwd + shifted(bwd, -1)


def _window_count(pos, length, w):
    lo = jnp.maximum(pos - w // 2, 0)
    hi = jnp.minimum(pos + (w - w // 2) - 1, length - 1)
    return (hi - lo + 1).astype(F32)


def _pool_kernel(x_ref, rstd_ref, g_ref, sc_ref, sh_ref, w_ref, ps_ref, g1_ref, o_ref):
    i = pl.program_id(0)
    grp = pl.program_id(1)
    x = x_ref[...]
    h = x * rstd_ref[...] * g_ref[...] * (1.0 + sc_ref[...]) + sh_ref[...]
    ridx = lax.broadcasted_iota(I32, (POOL_ROWS, 1), 0)

    def finish(pooled):
        p = (pooled - h).astype(BF16)
        y = _dot(p, w_ref[...]) * ps_ref[...]
        o_ref[...] = x + g1_ref[...] * y

    for gi, w in enumerate(POOL_WINDOWS):
        @pl.when((grp == gi) & (i * POOL_ROWS < NP_TOK))
        def _(w=w):
            pos = ridx & (SEQ - 1)
            s = _window_sum(h, pos, SEQ, 1, w)
            finish(s / _window_count(pos, SEQ, w))

        @pl.when((grp == gi) & (i * POOL_ROWS >= NP_TOK))
        def _(w=w):
            col = ridx & (GRID_W - 1)
            row = ridx // GRID_W
            n_rows = POOL_ROWS // GRID_W
            s = _window_sum(h, col, GRID_W, 1, w)
            s = _window_sum(s, row, n_rows, GRID_W, w)
            finish(s / (_window_count(col, GRID_W, w) * _window_count(row, n_rows, w)))


def _pool_mixer(x, rstd, g, sc, sh, pool_w, pool_scale, g1):
    ngrp = len(POOL_WINDOWS)
    mod = lambda i, j: (_mod_row(i * POOL_ROWS), 0, j)
    return pl.pallas_call(
        _pool_kernel,
        grid=(N_TOK // POOL_ROWS, ngrp),
        in_specs=[
            pl.BlockSpec((POOL_ROWS, POOL_GROUP), lambda i, j: (i, j)),
            pl.BlockSpec((POOL_ROWS, 1), lambda i, j: (i, 0)),
            pl.BlockSpec((1, POOL_GROUP), lambda i, j: (0, j)),
            pl.BlockSpec((None, 1, POOL_GROUP), mod),
            pl.BlockSpec((None, 1, POOL_GROUP), mod),
            pl.BlockSpec((None, POOL_GROUP, POOL_GROUP), lambda i, j: (j, 0, 0)),
            pl.BlockSpec((1, POOL_GROUP), lambda i, j: (0, j)),
            pl.BlockSpec((None, 1, POOL_GROUP), mod),
        ],
        out_specs=pl.BlockSpec((POOL_ROWS, POOL_GROUP), lambda i, j: (i, j)),
        out_shape=jax.ShapeDtypeStruct((N_TOK, D_MODEL), F32),
        compiler_params=_params(("parallel", "parallel")),
        name="pool_mixer",
    )(x, rstd, g, sc, sh, pool_w, pool_scale, g1)


def _moe_pre_kernel(x_ref, g_ref, sc_ref, sh_ref, rw_ref, rb_ref, hp_ref, idx_ref, wt_ref):
    tm = x_ref.shape[0]
    h = _norm_mod(x_ref[...], g_ref[...], sc_ref[...], sh_ref[...])
    hp_ref[...] = _pack_halves(h)
    logits = _nt(rw_ref[...], h, precision=lax.Precision.HIGHEST)
    s = jax.nn.sigmoid(logits)
    choice = s + rb_ref[...]
    neg = -jnp.inf
    row8 = lax.broadcasted_iota(I32, (GROUP_SIZE, tm), 0)
    rowg = lax.broadcasted_iota(I32, (N_EXPERT_GROUPS, tm), 0)

    def first_max(vals, riota, n):
        m = jnp.max(vals, axis=0, keepdims=True)
        i = jnp.min(jnp.where(vals == m, riota, n), axis=0, keepdims=True)
        return m, i

    gscore = jnp.zeros((N_EXPERT_GROUPS, tm), F32)
    for g in range(N_EXPERT_GROUPS):
        cg = choice[g * GROUP_SIZE:(g + 1) * GROUP_SIZE, :]
        m1, i1 = first_max(cg, row8, GROUP_SIZE)
        m2 = jnp.max(jnp.where(row8 == i1, neg, cg), axis=0, keepdims=True)
        gscore = jnp.where(rowg == g, m1 + m2, gscore)
    gsel = jnp.zeros((N_EXPERT_GROUPS, tm), F32)
    for _ in range(TOPK_GROUPS):
        _, i = first_max(gscore, rowg, N_EXPERT_GROUPS)
        gsel = jnp.where(rowg == i, 1.0, gsel)
        gscore = jnp.where(rowg == i, neg, gscore)
    masked = jnp.concatenate(
        [jnp.where(gsel[g:g + 1, :] > 0.0, choice[g * GROUP_SIZE:(g + 1) * GROUP_SIZE, :], neg)
         for g in range(N_EXPERT_GROUPS)], axis=0)
    rowe = lax.broadcasted_iota(I32, (N_EXPERTS, tm), 0)
    rowk = lax.broadcasted_iota(I32, (TOP_K, tm), 0)
    idx = jnp.zeros((TOP_K, tm), I32)
    wts = jnp.zeros((TOP_K, tm), F32)
    for kk in range(TOP_K):
        _, i = first_max(masked, rowe, N_EXPERTS)
        sel = rowe == i
        wk = jnp.sum(jnp.where(sel, s, 0.0), axis=0, keepdims=True)
        idx = jnp.where(rowk == kk, i, idx)
        wts = jnp.where(rowk == kk, wk, wts)
        masked = jnp.where(sel, neg, masked)
    wts = wts / jnp.sum(wts, axis=0, keepdims=True) * ROUTED_SCALE
    idx_ref[...] = idx
    wt_ref[...] = wts


def _moe_pre(x, g, sc, sh, router_wt, router_b):
    tm = 256
    mod_spec = pl.BlockSpec((None, 1, D_MODEL), lambda i: (_mod_row(i * tm), 0, 0))
    return pl.pallas_call(
        _moe_pre_kernel,
        grid=(N_TOK // tm,),
        in_specs=[
            pl.BlockSpec((tm, D_MODEL), lambda i: (i, 0)),
            pl.BlockSpec((1, D_MODEL), lambda i: (0, 0)),
            mod_spec,
            mod_spec,
            pl.BlockSpec((N_EXPERTS, D_MODEL), lambda i: (0, 0)),
            pl.BlockSpec((N_EXPERTS, 1), lambda i: (0, 0)),
        ],
        out_specs=[
            pl.BlockSpec((tm, HALF), lambda i: (i, 0)),
            pl.BlockSpec((TOP_K, tm), lambda i: (0, i)),
            pl.BlockSpec((TOP_K, tm), lambda i: (0, i)),
        ],
        out_shape=[
            jax.ShapeDtypeStruct((N_TOK, HALF), U32),
            jax.ShapeDtypeStruct((TOP_K, N_TOK), I32),
            jax.ShapeDtypeStruct((TOP_K, N_TOK), F32),
        ],
        compiler_params=_params(("parallel",)),
        name="moe_pre",
    )(x, g, sc, sh, router_wt, router_b)


def _route(idx_t):
    flat_e = idx_t.T.reshape(-1)
    iota = jnp.arange(N_ASSIGN, dtype=I32)
    e_sorted, order = lax.sort_key_val(flat_e, iota)
    _, inv_order = lax.sort_key_val(order, iota)
    start = jnp.searchsorted(e_sorted, jnp.arange(N_EXPERTS, dtype=I32), side="left").astype(I32)
    counts = jnp.concatenate([start[1:], jnp.full((1,), N_ASSIGN, I32)]) - start
    padded = (counts + MOE_BLK - 1) // MOE_BLK * MOE_BLK
    pend = jnp.cumsum(padded)
    pstart = pend - padded
    pos = pstart[flat_e] + inv_order - start[flat_e]
    block_start = jnp.arange(N_BLOCKS, dtype=I32) * MOE_BLK
    block_e = jnp.minimum(jnp.sum(pend[None, :] <= block_start[:, None], axis=1), N_EXPERTS - 1).astype(I32)
    slot = jnp.arange(N_SLOTS, dtype=I32)
    slot_e = jnp.repeat(block_e, MOE_BLK)
    within = slot - pstart[slot_e]
    valid = (within < counts[slot_e]) & (slot < pend[N_EXPERTS - 1])
    src = jnp.clip(start[slot_e] + within, 0, N_ASSIGN - 1)
    slot_tok = jnp.where(valid, order[src] // TOP_K, 0).astype(I32)
    n_used = (pend[N_EXPERTS - 1] // MOE_BLK).astype(I32).reshape(1)
    return slot_tok.reshape(N_BLOCKS, MOE_BLK), pos.astype(I32), block_e, n_used


def _gather_pipeline(step, n_steps, idx_hbm, src_hbm, idx_smem, bufs, sem_idx, sem_rows, n_rows):
    last = idx_hbm.shape[0] - 1

    def idx_copy(blk, slot):
        return pltpu.make_async_copy(idx_hbm.at[blk], idx_smem.at[slot], sem_idx.at[slot])

    def row_copy(slot, r, row):
        return pltpu.make_async_copy(src_hbm.at[pl.ds(row, 1)], bufs[slot].at[pl.ds(r, 1)], sem_rows.at[slot])

    def wait_rows(slot):
        pltpu.make_async_copy(src_hbm.at[pl.ds(0, n_rows)], bufs[slot], sem_rows.at[slot]).wait()

    @pl.when(step == 0)
    def _():
        idx_copy(0, 0).start()
        idx_copy(0, 0).wait()

        def one(r, carry):
            row_copy(0, r, idx_smem[0, r]).start()
            return carry
        lax.fori_loop(0, n_rows, one, 0, unroll=8)
        idx_copy(jnp.minimum(1, last), 1).start()

    def run(compute):
        for cur in range(2):
            nxt = 1 - cur

            @pl.when((step < n_steps) & (step % 2 == cur))
            def _(cur=cur, nxt=nxt):
                idx_copy(jnp.minimum(step + 1, last), nxt).wait()
                wait_rows(cur)
                for r in range(n_rows):
                    row_copy(nxt, r, idx_smem[nxt, r]).start()
                idx_copy(jnp.minimum(step + 2, last), cur).start()
                compute(bufs[cur])

                @pl.when(step == n_steps - 1)
                def _():
                    wait_rows(nxt)
                    idx_copy(last, cur).wait()

    return run


def _expert_kernel(be_ref, nu_ref, tok_hbm, hp_hbm, wgu_ref, wd_ref, y_ref, idx_smem, xbuf0, xbuf1, wgu_bf, wd_bf,
                   sem_idx, sem_rows):
    b = pl.program_id(0)
    nu = nu_ref[0]
    run = _gather_pipeline(b, nu, tok_hbm, hp_hbm, idx_smem, (xbuf0, xbuf1), sem_idx, sem_rows, MOE_BLK)

    @pl.when((b == 0) | (be_ref[b] != be_ref[jnp.maximum(b - 1, 0)]))
    def _():
        rows = 256

        def cast_gu(i, carry):
            r = pl.multiple_of(i * rows, rows)
            wgu_bf[pl.ds(r, rows), :] = wgu_ref[pl.ds(r, rows), :].astype(BF16)
            return carry
        lax.fori_loop(0, D_MODEL // rows, cast_gu, 0)

        def cast_d(i, carry):
            r = pl.multiple_of(i * rows, rows)
            wd_bf[pl.ds(r, rows), :] = wd_ref[pl.ds(r, rows), :].astype(BF16)
            return carry
        lax.fori_loop(0, D_EXPERT // rows, cast_d, 0)

    def compute(xbuf):
        xa, xb = _unpack_halves(xbuf[...])
        gu = _dot(xa.astype(BF16), wgu_bf[:HALF, :]) + _dot(xb.astype(BF16), wgu_bf[HALF:, :])
        act = (_silu(gu[:, :D_EXPERT]) * gu[:, D_EXPERT:]).astype(BF16)
        y_ref[...] = _pack_halves(_dot(act, wd_bf[...]))

    run(compute)

    @pl.when(b >= nu)
    def _():
        y_ref[...] = jnp.zeros(y_ref.shape, U32)


def _experts(slot_tok, block_e, n_used, hp, w_gu, w_down):
    grid_spec = pltpu.PrefetchScalarGridSpec(
        num_scalar_prefetch=2,
        grid=(N_BLOCKS,),
        in_specs=[
            pl.BlockSpec(memory_space=pl.ANY),
            pl.BlockSpec(memory_space=pl.ANY),
            pl.BlockSpec((None, D_MODEL, 2 * D_EXPERT), lambda b, be, nu: (be[b], 0, 0)),
            pl.BlockSpec((None, D_EXPERT, D_MODEL), lambda b, be, nu: (be[b], 0, 0)),
        ],
        out_specs=pl.BlockSpec((MOE_BLK, HALF), lambda b, be, nu: (b, 0)),
        scratch_shapes=[
            pltpu.SMEM((2, MOE_BLK), I32),
            pltpu.VMEM((MOE_BLK, HALF), U32),
            pltpu.VMEM((MOE_BLK, HALF), U32),
            pltpu.VMEM((D_MODEL, 2 * D_EXPERT), BF16),
            pltpu.VMEM((D_EXPERT, D_MODEL), BF16),
            pltpu.SemaphoreType.DMA((2,)),
            pltpu.SemaphoreType.DMA((2,)),
        ],
    )
    return pl.pallas_call(
        _expert_kernel,
        grid_spec=grid_spec,
        out_shape=jax.ShapeDtypeStruct((N_SLOTS, HALF), U32),
        compiler_params=_params(("arbitrary",)),
        name="moe_experts",
    )(block_e, n_used, slot_tok, hp, w_gu, w_down)


COMB_ROWS = COMB_TM * TOP_K


def _combine_kernel(pos_hbm, y_hbm, x_ref, hp_ref, wt_ref, g2_ref, sgu_ref, sd_ref, fg_ref, o_ref,
                    idx_smem, ybuf0, ybuf1, sem_idx, sem_rows, *, final_norm):
    run = _gather_pipeline(pl.program_id(0), pl.num_programs(0), pos_hbm, y_hbm, idx_smem, (ybuf0, ybuf1),
                           sem_idx, sem_rows, COMB_ROWS)

    def compute(ybuf):
        ha, hb = _unpack_halves(hp_ref[...])
        gu = _dot(ha.astype(BF16), sgu_ref[:HALF, :]) + _dot(hb.astype(BF16), sgu_ref[HALF:, :])
        act = (_silu(gu[:, :D_SHARED]) * gu[:, D_SHARED:]).astype(BF16)
        ffn = _dot(act, sd_ref[...])
        acc_a = ffn[:, :HALF]
        acc_b = ffn[:, HALF:]
        for k in range(TOP_K):
            ya, yb = _unpack_halves(ybuf[k * COMB_TM:(k + 1) * COMB_TM, :])
            wk = wt_ref[:, k:k + 1]
            acc_a = acc_a + wk * ya
            acc_b = acc_b + wk * yb
        xa = x_ref[:, :HALF] + g2_ref[:, :HALF] * acc_a
        xb = x_ref[:, HALF:] + g2_ref[:, HALF:] * acc_b
        if final_norm:
            ms = (jnp.sum(xa * xa, axis=-1, keepdims=True) + jnp.sum(xb * xb, axis=-1, keepdims=True)) / D_MODEL
            inv = lax.rsqrt(ms + EPS)
            xa = xa * inv * fg_ref[:, :HALF]
            xb = xb * inv * fg_ref[:, HALF:]
        o_ref[:, :HALF] = xa
        o_ref[:, HALF:] = xb

    run(compute)


def _combine(pos, y, x, hp, wts, g2, sh_gu, sh_down, final_g, *, final_norm):
    n_tiles = N_TOK // COMB_TM
    pos_t = pos.reshape(n_tiles, COMB_TM, TOP_K).transpose(0, 2, 1).reshape(n_tiles, COMB_ROWS)
    kern = functools.partial(_combine_kernel, final_norm=final_norm)
    return pl.pallas_call(
        kern,
        grid=(n_tiles,),
        in_specs=[
            pl.BlockSpec(memory_space=pl.ANY),
            pl.BlockSpec(memory_space=pl.ANY),
            pl.BlockSpec((COMB_TM, D_MODEL), lambda i: (i, 0)),
            pl.BlockSpec((COMB_TM, HALF), lambda i: (i, 0)),
            pl.BlockSpec((COMB_TM, TOP_K), lambda i: (i, 0)),
            pl.BlockSpec((None, 1, D_MODEL), lambda i: (_mod_row(i * COMB_TM), 0, 0)),
            pl.BlockSpec((D_MODEL, 2 * D_SHARED), lambda i: (0, 0)),
            pl.BlockSpec((D_SHARED, D_MODEL), lambda i: (0, 0)),
            pl.BlockSpec((1, D_MODEL), lambda i: (0, 0)),
        ],
        out_specs=pl.BlockSpec((COMB_TM, D_MODEL), lambda i: (i, 0)),
        out_shape=jax.ShapeDtypeStruct((N_TOK, D_MODEL), F32),
        scratch_shapes=[
            pltpu.SMEM((2, COMB_ROWS), I32),
            pltpu.VMEM((COMB_ROWS, HALF), U32),
            pltpu.VMEM((COMB_ROWS, HALF), U32),
            pltpu.SemaphoreType.DMA((2,)),
            pltpu.SemaphoreType.DMA((2,)),
        ],
        compiler_params=_params(("arbitrary",)),
        name="moe_combine",
    )(pos_t, y, x, hp, wts, g2, sh_gu, sh_down, final_g)


def _moe(x, g, sc, sh, g2, router_w, router_b, w_gu, w_down, sh_gu, sh_down, final_g, *, final_norm):
    hp, idx_t, wt_t = _moe_pre(x, g, sc, sh, router_w.T, router_b.reshape(N_EXPERTS, 1))
    slot_tok, pos, block_e, n_used = _route(idx_t)
    y = _experts(slot_tok, block_e, n_used, hp, w_gu, w_down)
    return _combine(pos, y, x, hp, wt_t.T, g2, sh_gu.astype(BF16), sh_down.astype(BF16),
                    final_g, final_norm=final_norm)


def kernel(x_prompt, x_sample, state_dn, c, c_ctx, ada_w, ada_b, norm_g, final_g, dn_w_in, dn_conv, dn_a_log,
           dn_dt_bias, dn_onorm_g, dn_w_out, pool_w, pool_scale, moe_router, moe_bias, moe_w_gu, moe_w_down,
           sh_w_gu, sh_w_down):
    x = jnp.concatenate([x_prompt.reshape(NP_TOK, D_MODEL), x_sample.reshape(NS_TOK, D_MODEL)], axis=0)
    cvec = jnp.concatenate([c_ctx[None, :], c, jnp.zeros((N_MOD - 1 - DEC_BATCH, D_MODEL), F32)], axis=0)
    mod = _ada_mod(cvec, ada_w, ada_b)
    mod = mod.reshape(mod.shape[0], N_MOD, 6, 1, D_MODEL)
    fg = final_g.reshape(1, D_MODEL)

    def mods(layer):
        return [mod[layer, :, p] for p in range(6)]

    sh1, sc1, g1, sh2, sc2, g2 = mods(0)
    o, proj, new_state = _deltanet_core(x, norm_g[0, 0].reshape(1, D_MODEL), sc1, sh1, dn_w_in[0], dn_conv[0],
                                        dn_a_log[0], dn_dt_bias[0], state_dn)
    x = _dn_out(o, proj, dn_onorm_g[0].reshape(1, DN_DV), dn_w_out[0].astype(BF16), x, g1)
    x = _moe(x, norm_g[0, 1].reshape(1, D_MODEL), sc2, sh2, g2, moe_router[0], moe_bias[0], moe_w_gu[0],
             moe_w_down[0], sh_w_gu[0], sh_w_down[0], fg, final_norm=False)

    sh1, sc1, g1, sh2, sc2, g2 = mods(1)
    x = _pool_mixer(x, _rstd(x), norm_g[1, 0].reshape(1, D_MODEL), sc1, sh1, pool_w[0].astype(BF16),
                    pool_scale[0].reshape(1, D_MODEL), g1)
    y = _moe(x, norm_g[1, 1].reshape(1, D_MODEL), sc2, sh2, g2, moe_router[1], moe_bias[1], moe_w_gu[1],
             moe_w_down[1], sh_w_gu[1], sh_w_down[1], fg, final_norm=True)

    y_prompt = y[:NP_TOK].reshape(BATCH, SEQ, D_MODEL)
    y_sample = y[NP_TOK:].reshape(DEC_BATCH, DEC_SEQ, D_MODEL)
    return (y_prompt, y_sample, new_state)
```

```python
import functools
import math

import jax
import jax.numpy as jnp
from jax import lax
from jax.experimental import pallas as pl
from jax.experimental.pallas import tpu as pltpu

F32 = jnp.float32
BF16 = jnp.bfloat16
I32 = jnp.int32
U32 = jnp.uint32

D_MODEL = 2048
BATCH = 16
SEQ = 256
DEC_BATCH = 8
DEC_SEQ = 2048
GRID_W = 64
NP_TOK = BATCH * SEQ
NS_TOK = DEC_BATCH * DEC_SEQ
N_TOK = NP_TOK + NS_TOK
N_MOD = 16

DN_DK = 128
DN_DV = 128
DN_NK = D_MODEL // DN_DK
DN_NV = 2 * DN_NK
DN_QK = DN_NK * DN_DK
DN_V = DN_NV * DN_DV
DN_CONV = 5
CHUNK = 64
DN_HKB = 4
N_GATE = 4 * DN_NV

POOL_WINDOWS = (2, 4, 8, 16)
POOL_GROUP = D_MODEL // len(POOL_WINDOWS)

N_EXPERTS = 64
TOP_K = 8
N_EXPERT_GROUPS = 8
GROUP_SIZE = N_EXPERTS // N_EXPERT_GROUPS
TOPK_GROUPS = 4
D_EXPERT = 512
D_SHARED = 512
ROUTED_SCALE = 2.5
EPS = 1e-6

MOE_BLK = 256
N_ASSIGN = N_TOK * TOP_K
N_BLOCKS = (N_ASSIGN + N_EXPERTS * (MOE_BLK - 1) + MOE_BLK - 1) // MOE_BLK
N_SLOTS = N_BLOCKS * MOE_BLK
HALF = D_MODEL // 2
COMB_TM = 128
VMEM_LIMIT = 56 * 1024 * 1024

_NT = (((1,), (1,)), ((), ()))
_TN = (((0,), (0,)), ((), ()))


def _nt(a, b, precision=None):
    return lax.dot_general(a, b, _NT, preferred_element_type=F32, precision=precision)


def _tn(a, b):
    return lax.dot_general(a, b, _TN, preferred_element_type=F32)


def _dot(a, b, precision=None):
    return jnp.dot(a, b, preferred_element_type=F32, precision=precision)


def _silu(x):
    return x * jax.nn.sigmoid(x)


def _mod_row(tile_start):
    return jnp.where(tile_start < NP_TOK, 0, 1 + (tile_start - NP_TOK) // DEC_SEQ)


def _params(sem, vmem=VMEM_LIMIT):
    return pltpu.CompilerParams(dimension_semantics=sem, vmem_limit_bytes=vmem)


def _pack_halves(y):
    a = lax.bitcast_convert_type(y[:, :HALF].astype(BF16).astype(F32), U32)
    b = lax.bitcast_convert_type(y[:, HALF:].astype(BF16).astype(F32), U32)
    return a | (b >> 16)


def _unpack_halves(u):
    a = lax.bitcast_convert_type(u & jnp.uint32(0xFFFF0000), F32)
    b = lax.bitcast_convert_type(u << 16, F32)
    return a, b


def _ada_kernel(c_ref, w_ref, b_ref, o_ref):
    a = _silu(c_ref[...]).astype(BF16)
    o_ref[...] = _dot(a, w_ref[...].astype(BF16)) + b_ref[...]


def _ada_mod(cvec, ada_w, ada_b):
    depth, d, n = ada_w.shape
    tn = 1024
    return pl.pallas_call(
        _ada_kernel,
        grid=(depth, n // tn),
        in_specs=[
            pl.BlockSpec((N_MOD, d), lambda l, j: (0, 0)),
            pl.BlockSpec((None, d, tn), lambda l, j: (l, 0, j)),
            pl.BlockSpec((None, 1, tn), lambda l, j: (l, 0, j)),
        ],
        out_specs=pl.BlockSpec((None, N_MOD, tn), lambda l, j: (l, 0, j)),
        out_shape=jax.ShapeDtypeStruct((depth, N_MOD, n), F32),
        compiler_params=_params(("parallel", "parallel")),
        name="ada_mod",
    )(cvec, ada_w, ada_b.reshape(depth, 1, n))


def _norm_mod(x, g, sc, sh):
    ms = jnp.mean(x * x, axis=-1, keepdims=True)
    return x * lax.rsqrt(ms + EPS) * g * (1.0 + sc) + sh


def _in_proj_kernel(x_ref, g_ref, sc_ref, sh_ref, w_ref, wab_ref, wab_lo_ref, o_ref, ab_ref, h_scr):
    @pl.when(pl.program_id(1) == 0)
    def _():
        hf = _norm_mod(x_ref[...], g_ref[...], sc_ref[...], sh_ref[...])
        h = hf.astype(BF16)
        h_scr[...] = h
        h_lo = (hf - h.astype(F32)).astype(BF16)
        ab_ref[...] = _dot(h, wab_ref[...]) + (_dot(h_lo, wab_ref[...]) + _dot(h, wab_lo_ref[...]))

    o_ref[...] = _dot(h_scr[...], w_ref[...]).astype(o_ref.dtype)


def _in_proj(x, g, sc, sh, w, wab_f32):
    wab = wab_f32.astype(BF16)
    wab_lo = (wab_f32 - wab.astype(F32)).astype(BF16)
    tm, tn = 1024, 512
    n = w.shape[1]
    mod_spec = pl.BlockSpec((None, 1, D_MODEL), lambda i, j: (_mod_row(i * tm), 0, 0))
    return pl.pallas_call(
        _in_proj_kernel,
        grid=(N_TOK // tm, n // tn),
        in_specs=[
            pl.BlockSpec((tm, D_MODEL), lambda i, j: (i, 0)),
            pl.BlockSpec((1, D_MODEL), lambda i, j: (0, 0)),
            mod_spec,
            mod_spec,
            pl.BlockSpec((D_MODEL, tn), lambda i, j: (0, j)),
            pl.BlockSpec((D_MODEL, N_GATE), lambda i, j: (0, 0)),
            pl.BlockSpec((D_MODEL, N_GATE), lambda i, j: (0, 0)),
        ],
        out_specs=[
            pl.BlockSpec((tm, tn), lambda i, j: (i, j)),
            pl.BlockSpec((tm, N_GATE), lambda i, j: (i, 0)),
        ],
        out_shape=[
            jax.ShapeDtypeStruct((N_TOK, n), BF16),
            jax.ShapeDtypeStruct((N_TOK, N_GATE), F32),
        ],
        scratch_shapes=[pltpu.VMEM((tm, D_MODEL), BF16)],
        compiler_params=_params(("parallel", "arbitrary")),
        name="dn_in_proj",
    )(x, g, sc, sh, w, wab, wab_lo)


def _gates_kernel(ab_ref, alog_ref, dt_ref, o_ref):
    tm = ab_ref.shape[0]
    a = ab_ref[...]
    z = a + dt_ref[...]
    softplus = jnp.maximum(z, 0.0) + jnp.log(1.0 + jnp.exp(-jnp.abs(z)))
    g = -jnp.exp(alog_ref[...]) * softplus
    beta = jax.nn.sigmoid(a)
    ii = lax.broadcasted_iota(I32, (tm, tm), 0)
    jj = lax.broadcasted_iota(I32, (tm, tm), 1)
    same = (ii // CHUNK) == (jj // CHUNK)
    m_f = jnp.where(same & (jj <= ii), 1.0, 0.0).astype(F32)
    m_b = jnp.where(same & (jj >= ii), 1.0, 0.0).astype(F32)
    gc_f = _dot(m_f, g, precision=lax.Precision.HIGHEST)
    gc_b = _dot(m_b, g, precision=lax.Precision.HIGHEST)
    lane = lax.broadcasted_iota(I32, a.shape, 1)
    is_g = (lane % (2 * DN_NV)) < DN_NV
    o_ref[...] = jnp.where(is_g, jnp.where(lane < 2 * DN_NV, gc_f, gc_b), beta)


def _gates(ab, a_log, dt_bias):
    tm = 256
    zeros = jnp.zeros((DN_NV,), F32)
    alog = jnp.concatenate([a_log[0], zeros, a_log[1], zeros]).reshape(1, N_GATE)
    dt = jnp.concatenate([dt_bias[0], zeros, dt_bias[1], zeros]).reshape(1, N_GATE)
    return pl.pallas_call(
        _gates_kernel,
        grid=(N_TOK // tm,),
        in_specs=[
            pl.BlockSpec((tm, N_GATE), lambda i: (i, 0)),
            pl.BlockSpec((1, N_GATE), lambda i: (0, 0)),
            pl.BlockSpec((1, N_GATE), lambda i: (0, 0)),
        ],
        out_specs=pl.BlockSpec((tm, N_GATE), lambda i: (i, 0)),
        out_shape=jax.ShapeDtypeStruct((N_TOK, N_GATE), F32),
        compiler_params=_params(("parallel",)),
        name="dn_gates",
    )(ab, alog, dt)


CONV_ROWS = 2048
CONV_TC = 512


CONV_CHUNK = 128
CONV_EDGE = 16


def _conv_kernel(x_ref, w_ref, o_ref):
    i = pl.program_id(0)
    j = pl.program_id(1)
    rows = x_ref.shape[0]
    seq_len = jnp.where(i * rows < NP_TOK, SEQ, DEC_SEQ)
    half = DN_CONV // 2
    taps = [s for s in range(-half, half + 1) if s != 0]
    win_rows = CONV_CHUNK + 2 * CONV_EDGE
    q = lax.broadcasted_iota(I32, (len(taps) * CONV_CHUNK, win_rows), 0)
    col = lax.broadcasted_iota(I32, (len(taps) * CONV_CHUNK, win_rows), 1)
    src = CONV_EDGE + (q % CONV_CHUNK)
    for t, s in enumerate(taps):
        src = jnp.where(q // CONV_CHUNK == t, src + s, src)
    shift_mat = jnp.where(col == src, 1.0, 0.0).astype(BF16)
    is_qk = j < (2 * DN_QK) // CONV_TC
    qscale = jnp.where(j < DN_QK // CONV_TC, DN_DK ** -0.5, 1.0)

    def chunk(c, carry):
        r0 = pl.multiple_of(c * CONV_CHUNK, CONV_CHUNK)
        xc = x_ref[pl.ds(r0, CONV_CHUNK), :]
        r_prev = pl.multiple_of(jnp.maximum(r0 - CONV_EDGE, 0), CONV_EDGE)
        r_next = pl.multiple_of(jnp.minimum(r0 + CONV_CHUNK, rows - CONV_EDGE), CONV_EDGE)
        prev = x_ref[pl.ds(r_prev, CONV_EDGE), :]
        nxt = x_ref[pl.ds(r_next, CONV_EDGE), :]
        prev = jnp.where((r0 & (seq_len - 1)) != 0, prev, jnp.zeros_like(prev))
        nxt = jnp.where(((r0 + CONV_CHUNK) & (seq_len - 1)) != 0, nxt, jnp.zeros_like(nxt))
        shifted = _dot(shift_mat, jnp.concatenate([prev, xc, nxt], axis=0))
        acc = xc.astype(F32) * w_ref[half:half + 1, :]
        for t, s in enumerate(taps):
            acc = acc + shifted[t * CONV_CHUNK:(t + 1) * CONV_CHUNK] * w_ref[half + s:half + s + 1, :]
        y = _silu(acc)
        for h in range(CONV_TC // DN_DK):
            sl = slice(h * DN_DK, (h + 1) * DN_DK)
            yh = y[:, sl]
            inv = lax.rsqrt(jnp.sum(yh * yh, axis=-1, keepdims=True) + EPS) * qscale
            o_ref[pl.ds(r0, CONV_CHUNK), sl] = (yh * jnp.where(is_qk, inv, 1.0)).astype(o_ref.dtype)
        return carry

    lax.fori_loop(0, rows // CONV_CHUNK, chunk, 0, unroll=4)


def _conv_qkv(proj, conv_w):
    nch = 2 * DN_QK + DN_V
    return pl.pallas_call(
        _conv_kernel,
        grid=(N_TOK // CONV_ROWS, nch // CONV_TC),
        in_specs=[
            pl.BlockSpec((CONV_ROWS, CONV_TC), lambda i, j: (i, j)),
            pl.BlockSpec((DN_CONV, CONV_TC), lambda i, j: (0, j)),
        ],
        out_specs=pl.BlockSpec((CONV_ROWS, CONV_TC), lambda i, j: (i, j)),
        out_shape=jax.ShapeDtypeStruct((N_TOK, nch), BF16),
        compiler_params=_params(("parallel", "parallel")),
        name="dn_conv",
    )(proj, conv_w)


def _delta_kernel(*refs, n_chunks, has_s0, write_state):
    q_ref, k_ref, v_ref, gc_ref, rows_ref = refs[:5]
    pos = 5
    if has_s0:
        s0_ref = refs[pos]
        pos += 1
    o_ref = refs[pos]
    pos += 1
    if write_state:
        sfin_ref = refs[pos]
        pos += 1
    s_scr = refs[pos]
    out_ref = o_ref
    o_ref = refs[pos + 1]

    hk0 = pl.program_id(1) * DN_HKB
    if has_s0:
        for hl in range(DN_HKB):
            for d in range(2):
                for r in range(2):
                    s_scr[4 * hl + 2 * d + r] = s0_ref[d, 2 * hl + r]
    else:
        s_scr[...] = jnp.zeros(s_scr.shape, F32)
    o_ref[...] = jnp.zeros(o_ref.shape, F32)

    ii = lax.broadcasted_iota(I32, (CHUNK, CHUNK), 0)
    jj = lax.broadcasted_iota(I32, (CHUNK, CHUNK), 1)
    eye = jnp.where(ii == jj, 1.0, 0.0).astype(F32)
    lane = lax.broadcasted_iota(I32, (CHUNK, N_GATE), 1)
    pair_masks = [((ii >> (l + 1)) == (jj >> (l + 1))) & ((ii >> l) != (jj >> l))
                  for l in range(int(math.log2(CHUNK)))]

    def body(t, carry):
        chains = []
        for hl in range(DN_HKB):
            for d in range(2):
                c = t if d == 0 else n_chunks - 1 - t
                r0 = pl.multiple_of(c * CHUNK, CHUNK)
                k = k_ref[pl.ds(r0, CHUNK), hl * DN_DK:(hl + 1) * DN_DK]
                q = q_ref[pl.ds(r0, CHUNK), hl * DN_DK:(hl + 1) * DN_DK]
                gates = gc_ref[pl.ds(r0, CHUNK), :]
                rows = rows_ref[hl, c]
                kk = _nt(k, k)
                qk = _nt(q, k)
                incl = (jj <= ii) if d == 0 else (jj >= ii)
                strict = (jj < ii) if d == 0 else (jj > ii)
                last = CHUNK - 1 if d == 0 else 0
                for r in range(2):
                    lane_g = d * 2 * DN_NV + 2 * (hk0 + hl) + r
                    gcc = jnp.sum(jnp.where(lane == lane_g, gates, 0.0), axis=1, keepdims=True)
                    bc = jnp.sum(jnp.where(lane == lane_g + DN_NV, gates, 0.0), axis=1, keepdims=True)
                    gcr = rows[2 * d + r:2 * d + r + 1, :]
                    gl = gcr[:, last:last + 1]
                    dec = jnp.where(incl, jnp.exp(jnp.where(incl, gcc - gcr, 0.0)), 0.0)
                    lm = jnp.where(strict, bc * kk * dec, 0.0)
                    v = v_ref[pl.ds(r0, CHUNK), (2 * hl + r) * DN_DV:(2 * hl + r + 1) * DN_DV]
                    chains.append(dict(k=k, q=q, v=v, r0=r0, si=4 * hl + 2 * d + r, oc=2 * hl + r,
                                       gcc=gcc, bc=bc, gl=gl, attn=(qk * dec).astype(BF16),
                                       lb=lm.astype(BF16), tm=eye - jnp.where(pair_masks[0], lm, 0.0)))
        for lvl in range(1, len(pair_masks)):
            for ch in chains:
                ch["tb"] = ch["tm"].astype(BF16)
                ch["bt"] = _dot(jnp.where(pair_masks[lvl], ch["lb"], jnp.zeros_like(ch["lb"])), ch["tb"])
            for ch in chains:
                ch["tm"] = ch["tm"] - _dot(ch["tb"], ch["bt"].astype(BF16))
        for ch in chains:
            kf = ch["k"].astype(F32)
            egc = jnp.exp(ch["gcc"])
            rhs = jnp.concatenate([(ch["v"].astype(F32) * ch["bc"]).astype(BF16),
                                   (kf * (ch["bc"] * egc)).astype(BF16)], axis=1)
            ch["uw"] = _dot(ch["tm"].astype(BF16), rhs)
            ch["qg"] = (ch["q"].astype(F32) * egc).astype(BF16)
            ch["kd"] = (kf * jnp.exp(ch["gl"] - ch["gcc"])).astype(BF16)
        for ch in chains:
            ch["s"] = s_scr[ch["si"]]
            wq = jnp.concatenate([ch["uw"][:, DN_DV:].astype(BF16), ch["qg"]], axis=0)
            ch["ws_qs"] = _dot(wq, ch["s"].astype(BF16))
        for ch in chains:
            v_new = (ch["uw"][:, :DN_DV] - ch["ws_qs"][:CHUNK]).astype(BF16)
            o = ch["ws_qs"][CHUNK:] + _dot(ch["attn"], v_new)
            s_scr[ch["si"]] = ch["s"] * jnp.exp(ch["gl"]) + _tn(ch["kd"], v_new)
            o_ref[pl.ds(ch["r0"], CHUNK), ch["oc"] * DN_DV:(ch["oc"] + 1) * DN_DV] += o
        return carry

    lax.fori_loop(0, n_chunks, body, 0)

    def emit(c, carry):
        r0 = pl.multiple_of(c * CHUNK, CHUNK)
        out_ref[pl.ds(r0, CHUNK), :] = o_ref[pl.ds(r0, CHUNK), :].astype(out_ref.dtype)
        return carry

    lax.fori_loop(0, n_chunks, emit, 0)
    if write_state:
        for hl in range(DN_HKB):
            for d in range(2):
                for r in range(2):
                    sfin_ref[d, 2 * hl + r] = s_scr[4 * hl + 2 * d + r]


def _delta(qkv, gc, rows, s0, *, n_seq, seq_len, row_blk0, write_state):
    n_chunks = seq_len // CHUNK
    has_s0 = s0 is not None
    kern = functools.partial(_delta_kernel, n_chunks=n_chunks, has_s0=has_s0, write_state=write_state)
    ngrp = DN_NK // DN_HKB
    qk_w = DN_HKB * DN_DK
    v_w = 2 * DN_HKB * DN_DV
    in_specs = [
        pl.BlockSpec((seq_len, qk_w), lambda b, h: (row_blk0 + b, h)),
        pl.BlockSpec((seq_len, qk_w), lambda b, h: (row_blk0 + b, ngrp + h)),
        pl.BlockSpec((seq_len, v_w), lambda b, h: (row_blk0 + b, ngrp + h)),
        pl.BlockSpec((seq_len, N_GATE), lambda b, h: (row_blk0 + b, 0)),
        pl.BlockSpec((None, DN_HKB, n_chunks, 8, CHUNK), lambda b, h: (b, h, 0, 0, 0)),
    ]
    args = [qkv, qkv, qkv, gc, rows]
    state_spec = pl.BlockSpec((None, None, 2, 2 * DN_HKB, DN_DK, DN_DV), lambda b, h: (b, 0, 0, h, 0, 0))
    if has_s0:
        in_specs.append(state_spec)
        args.append(s0)
    out_specs = [pl.BlockSpec((seq_len, v_w), lambda b, h: (b, h))]
    out_shape = [jax.ShapeDtypeStruct((n_seq * seq_len, DN_V), BF16)]
    if write_state:
        out_specs.append(state_spec)
        out_shape.append(jax.ShapeDtypeStruct((n_seq, 1, 2, DN_NV, DN_DK, DN_DV), F32))
    return pl.pallas_call(
        kern,
        grid=(n_seq, ngrp),
        in_specs=in_specs,
        out_specs=out_specs,
        out_shape=out_shape,
        scratch_shapes=[pltpu.VMEM((4 * DN_HKB, DN_DK, DN_DV), F32), pltpu.VMEM((seq_len, v_w), F32)],
        compiler_params=_params(("parallel", "parallel")),
        name="dn_delta_%d" % seq_len,
    )(*args)


def _gate_rows(gc, n_seq, seq_len):
    n_chunks = seq_len // CHUNK
    g = gc.reshape(n_seq, n_chunks, CHUNK, 2, 2, DN_NK, 2)
    g = g.transpose(0, 5, 1, 4, 3, 6, 2)
    return g.reshape(n_seq, DN_NK, n_chunks, 8, CHUNK)


def _deltanet_core(x, g, sc1, sh1, w_in, conv_w, a_log, dt_bias, state_dn, sample=True):
    nqkv = 2 * DN_QK + 2 * DN_V
    proj, ab = _in_proj(x, g, sc1, sh1, w_in[:, :nqkv].astype(BF16), w_in[:, nqkv:])
    gc = _gates(ab, a_log, dt_bias)
    qkv = _conv_qkv(proj, conv_w)
    o_p, new_state = _delta(qkv, gc, _gate_rows(gc[:NP_TOK], BATCH, SEQ), None,
                            n_seq=BATCH, seq_len=SEQ, row_blk0=0, write_state=True)
    if not sample:
        return o_p, proj, new_state
    (o_s,) = _delta(qkv, gc, _gate_rows(gc[NP_TOK:], DEC_BATCH, DEC_SEQ), state_dn,
                    n_seq=DEC_BATCH, seq_len=DEC_SEQ, row_blk0=NP_TOK // DEC_SEQ, write_state=False)
    return jnp.concatenate([o_p, o_s], axis=0), proj, new_state


def _dn_out_kernel(o_ref, z_ref, og_ref, w_ref, x_ref, g1_ref, out_ref, a_scr):
    @pl.when(pl.program_id(1) == 0)
    def _():
        for h in range(DN_NV):
            sl = slice(h * DN_DV, (h + 1) * DN_DV)
            o = o_ref[:, sl].astype(F32)
            z = z_ref[:, sl].astype(F32)
            on = o * lax.rsqrt(jnp.mean(o * o, axis=-1, keepdims=True) + EPS) * og_ref[...]
            a_scr[:, sl] = (on * _silu(z)).astype(BF16)

    out_ref[...] = x_ref[...] + g1_ref[...] * _dot(a_scr[...], w_ref[...])


def _dn_out(o, proj, onorm_g, w_out, x, g1):
    tm, tn = 512, 1024
    zblk = (2 * DN_QK + DN_V) // DN_V
    return pl.pallas_call(
        _dn_out_kernel,
        grid=(N_TOK // tm, D_MODEL // tn),
        in_specs=[
            pl.BlockSpec((tm, DN_V), lambda i, j: (i, 0)),
            pl.BlockSpec((tm, DN_V), lambda i, j: (i, zblk)),
            pl.BlockSpec((1, DN_DV), lambda i, j: (0, 0)),
            pl.BlockSpec((DN_V, tn), lambda i, j: (0, j)),
            pl.BlockSpec((tm, tn), lambda i, j: (i, j)),
            pl.BlockSpec((None, 1, tn), lambda i, j: (_mod_row(i * tm), 0, j)),
        ],
        out_specs=pl.BlockSpec((tm, tn), lambda i, j: (i, j)),
        out_shape=jax.ShapeDtypeStruct((N_TOK, D_MODEL), F32),
        scratch_shapes=[pltpu.VMEM((tm, DN_V), BF16)],
        compiler_params=_params(("parallel", "arbitrary")),
        name="dn_out",
    )(o, proj, onorm_g, w_out, x, g1)


def _rstd_kernel(x_ref, o_ref):
    x = x_ref[...]
    o_ref[...] = lax.rsqrt(jnp.mean(x * x, axis=-1, keepdims=True) + EPS)


def _rstd(x):
    tm = 1024
    return pl.pallas_call(
        _rstd_kernel,
        grid=(N_TOK // tm,),
        in_specs=[pl.BlockSpec((tm, D_MODEL), lambda i: (i, 0))],
        out_specs=pl.BlockSpec((tm, 1), lambda i: (i, 0)),
        out_shape=jax.ShapeDtypeStruct((N_TOK, 1), F32),
        compiler_params=_params(("parallel",)),
        name="rstd",
    )(x)


POOL_ROWS = 2048


def _window_sum(x, pos, length, stride, w):
    rows = x.shape[0]
    m = w // 2

    def shifted(a, off):
        valid = (pos + off >= 0) & (pos + off < length)
        return jnp.where(valid, pltpu.roll(a, (-off * stride) % rows, 0), 0.0)

    fwd = x
    bwd = x
    step = 1
    while step < m:
        fwd = fwd + shifted(fwd, step)
        bwd = bwd + shifted(bwd, -step)
        step *= 2
    return fwd + shifted(bwd, -1)


def _window_count(pos, length, w):
    lo = jnp.maximum(pos - w // 2, 0)
    hi = jnp.minimum(pos + (w - w // 2) - 1, length - 1)
    return (hi - lo + 1).astype(F32)


def _pool_kernel(x_ref, rstd_ref, g_ref, sc_ref, sh_ref, w_ref, ps_ref, g1_ref, o_ref):
    i = pl.program_id(0)
    grp = pl.program_id(1)
    x = x_ref[...]
    h = x * rstd_ref[...] * g_ref[...] * (1.0 + sc_ref[...]) + sh_ref[...]
    ridx = lax.broadcasted_iota(I32, (POOL_ROWS, 1), 0)

    def finish(pooled):
        p = (pooled - h).astype(BF16)
        y = _dot(p, w_ref[...]) * ps_ref[...]
        o_ref[...] = x + g1_ref[...] * y

    for gi, w in enumerate(POOL_WINDOWS):
        @pl.when((grp == gi) & (i * POOL_ROWS < NP_TOK))
        def _(w=w):
            pos = ridx & (SEQ - 1)
            s = _window_sum(h, pos, SEQ, 1, w)
            finish(s / _window_count(pos, SEQ, w))

        @pl.when((grp == gi) & (i * POOL_ROWS >= NP_TOK))
        def _(w=w):
            col = ridx & (GRID_W - 1)
            row = ridx // GRID_W
            n_rows = POOL_ROWS // GRID_W
            s = _window_sum(h, col, GRID_W, 1, w)
            s = _window_sum(s, row, n_rows, GRID_W, w)
            finish(s / (_window_count(col, GRID_W, w) * _window_count(row, n_rows, w)))


def _pool_mixer(x, rstd, g, sc, sh, pool_w, pool_scale, g1):
    ngrp = len(POOL_WINDOWS)
    mod = lambda i, j: (_mod_row(i * POOL_ROWS), 0, j)
    return pl.pallas_call(
        _pool_kernel,
        grid=(N_TOK // POOL_ROWS, ngrp),
        in_specs=[
            pl.BlockSpec((POOL_ROWS, POOL_GROUP), lambda i, j: (i, j)),
            pl.BlockSpec((POOL_ROWS, 1), lambda i, j: (i, 0)),
            pl.BlockSpec((1, POOL_GROUP), lambda i, j: (0, j)),
            pl.BlockSpec((None, 1, POOL_GROUP), mod),
            pl.BlockSpec((None, 1, POOL_GROUP), mod),
            pl.BlockSpec((None, POOL_GROUP, POOL_GROUP), lambda i, j: (j, 0, 0)),
            pl.BlockSpec((1, POOL_GROUP), lambda i, j: (0, j)),
            pl.BlockSpec((None, 1, POOL_GROUP), mod),
        ],
        out_specs=pl.BlockSpec((POOL_ROWS, POOL_GROUP), lambda i, j: (i, j)),
        out_shape=jax.ShapeDtypeStruct((N_TOK, D_MODEL), F32),
        compiler_params=_params(("parallel", "parallel")),
        name="pool_mixer",
    )(x, rstd, g, sc, sh, pool_w, pool_scale, g1)


def _moe_pre_kernel(x_ref, g_ref, sc_ref, sh_ref, rw_ref, rb_ref, hp_ref, idx_ref, wt_ref):
    tm = x_ref.shape[0]
    h = _norm_mod(x_ref[...], g_ref[...], sc_ref[...], sh_ref[...])
    hp_ref[...] = _pack_halves(h)
    logits = _nt(rw_ref[...], h, precision=lax.Precision.HIGHEST)
    s = jax.nn.sigmoid(logits)
    choice = s + rb_ref[...]
    neg = -jnp.inf
    row8 = lax.broadcasted_iota(I32, (GROUP_SIZE, tm), 0)
    rowg = lax.broadcasted_iota(I32, (N_EXPERT_GROUPS, tm), 0)

    def first_max(vals, riota, n):
        m = jnp.max(vals, axis=0, keepdims=True)
        i = jnp.min(jnp.where(vals == m, riota, n), axis=0, keepdims=True)
        return m, i

    gscore = jnp.zeros((N_EXPERT_GROUPS, tm), F32)
    for g in range(N_EXPERT_GROUPS):
        cg = choice[g * GROUP_SIZE:(g + 1) * GROUP_SIZE, :]
        m1, i1 = first_max(cg, row8, GROUP_SIZE)
        m2 = jnp.max(jnp.where(row8 == i1, neg, cg), axis=0, keepdims=True)
        gscore = jnp.where(rowg == g, m1 + m2, gscore)
    gsel = jnp.zeros((N_EXPERT_GROUPS, tm), F32)
    for _ in range(TOPK_GROUPS):
        _, i = first_max(gscore, rowg, N_EXPERT_GROUPS)
        gsel = jnp.where(rowg == i, 1.0, gsel)
        gscore = jnp.where(rowg == i, neg, gscore)
    masked = jnp.concatenate(
        [jnp.where(gsel[g:g + 1, :] > 0.0, choice[g * GROUP_SIZE:(g + 1) * GROUP_SIZE, :], neg)
         for g in range(N_EXPERT_GROUPS)], axis=0)
    rowe = lax.broadcasted_iota(I32, (N_EXPERTS, tm), 0)
    rowk = lax.broadcasted_iota(I32, (TOP_K, tm), 0)
    idx = jnp.zeros((TOP_K, tm), I32)
    wts = jnp.zeros((TOP_K, tm), F32)
    for kk in range(TOP_K):
        _, i = first_max(masked, rowe, N_EXPERTS)
        sel = rowe == i
        wk = jnp.sum(jnp.where(sel, s, 0.0), axis=0, keepdims=True)
        idx = jnp.where(rowk == kk, i, idx)
        wts = jnp.where(rowk == kk, wk, wts)
        masked = jnp.where(sel, neg, masked)
    wts = wts / jnp.sum(wts, axis=0, keepdims=True) * ROUTED_SCALE
    idx_ref[...] = idx
    wt_ref[...] = wts


def _moe_pre(x, g, sc, sh, router_wt, router_b):
    tm = 256
    mod_spec = pl.BlockSpec((None, 1, D_MODEL), lambda i: (_mod_row(i * tm), 0, 0))
    return pl.pallas_call(
        _moe_pre_kernel,
        grid=(N_TOK // tm,),
        in_specs=[
            pl.BlockSpec((tm, D_MODEL), lambda i: (i, 0)),
            pl.BlockSpec((1, D_MODEL), lambda i: (0, 0)),
            mod_spec,
            mod_spec,
            pl.BlockSpec((N_EXPERTS, D_MODEL), lambda i: (0, 0)),
            pl.BlockSpec((N_EXPERTS, 1), lambda i: (0, 0)),
        ],
        out_specs=[
            pl.BlockSpec((tm, HALF), lambda i: (i, 0)),
            pl.BlockSpec((TOP_K, tm), lambda i: (0, i)),
            pl.BlockSpec((TOP_K, tm), lambda i: (0, i)),
        ],
        out_shape=[
            jax.ShapeDtypeStruct((N_TOK, HALF), U32),
            jax.ShapeDtypeStruct((TOP_K, N_TOK), I32),
            jax.ShapeDtypeStruct((TOP_K, N_TOK), F32),
        ],
        compiler_params=_params(("parallel",)),
        name="moe_pre",
    )(x, g, sc, sh, router_wt, router_b)


def _route(idx_t):
    flat_e = idx_t.T.reshape(-1)
    iota = jnp.arange(N_ASSIGN, dtype=I32)
    e_sorted, order = lax.sort_key_val(flat_e, iota)
    _, inv_order = lax.sort_key_val(order, iota)
    start = jnp.searchsorted(e_sorted, jnp.arange(N_EXPERTS, dtype=I32), side="left", method="scan").astype(I32)
    counts = jnp.concatenate([start[1:], jnp.full((1,), N_ASSIGN, I32)]) - start
    padded = (counts + MOE_BLK - 1) // MOE_BLK * MOE_BLK
    pend = jnp.cumsum(padded)
    pstart = pend - padded
    shift = pstart - start
    dshift = shift - jnp.concatenate([jnp.zeros((1,), I32), shift[:-1]])
    steps = jnp.zeros((N_ASSIGN,), I32).at[start].add(dshift, mode="drop")
    dest_sorted = iota + jnp.cumsum(steps)
    pos = dest_sorted[inv_order]
    block_start = jnp.arange(N_BLOCKS, dtype=I32) * MOE_BLK
    block_e = jnp.minimum(jnp.sum(pend[None, :] <= block_start[:, None], axis=1), N_EXPERTS - 1).astype(I32)
    slot = block_start[:, None] + jnp.arange(MOE_BLK, dtype=I32)[None, :]
    within = slot - pstart[block_e][:, None]
    valid = (within < counts[block_e][:, None]) & (slot < pend[N_EXPERTS - 1])
    src = jnp.clip(slot - shift[block_e][:, None], 0, N_ASSIGN - 1)
    slot_tok = jnp.where(valid, order[src] // TOP_K, 0).astype(I32)
    n_used = (pend[N_EXPERTS - 1] // MOE_BLK).astype(I32).reshape(1)
    return slot_tok.reshape(N_BLOCKS, MOE_BLK), pos.astype(I32), block_e, n_used


GATHER_DEPTH = 3
GATHER_IDX_ROWS = 4


def _gather_pipeline(step, n_steps, idx_hbm, src_hbm, idx_smem, bufs, sem_idx, sem_rows, n_rows):
    last = idx_hbm.shape[0] - 1
    nbuf = GATHER_DEPTH

    def idx_copy(blk, slot):
        return pltpu.make_async_copy(idx_hbm.at[jnp.minimum(blk, last)], idx_smem.at[slot], sem_idx.at[slot])

    def row_copy(slot, r, row):
        return pltpu.make_async_copy(src_hbm.at[pl.ds(row, 1)], bufs[slot].at[pl.ds(r, 1)], sem_rows.at[slot])

    def wait_rows(slot):
        pltpu.make_async_copy(src_hbm.at[pl.ds(0, n_rows)], bufs[slot], sem_rows.at[slot]).wait()

    @pl.when(step == 0)
    def _():
        for s in range(nbuf - 1):
            idx_copy(s, s).start()
            idx_copy(s, s).wait()

            def one(r, carry, s=s):
                row_copy(s, r, idx_smem[s, r]).start(priority=s % 2)
                return carry
            lax.fori_loop(0, n_rows, one, 0, unroll=8)
        idx_copy(nbuf - 1, nbuf - 1).start()

    def run(compute):
        for cur in range(nbuf):
            new = (cur + nbuf - 1) % nbuf

            @pl.when((step < n_steps) & (step % nbuf == cur))
            def _(cur=cur, new=new):
                idx_copy(step + nbuf - 1, new).wait()
                wait_rows(cur)
                for r in range(n_rows):
                    row_copy(new, r, idx_smem[new, r]).start(priority=r % 2)
                idx_copy(step + nbuf, cur).start()
                compute(bufs[cur])

                @pl.when(step == n_steps - 1)
                def _():
                    for s in range(1, nbuf):
                        wait_rows((cur + s) % nbuf)
                    idx_copy(last, cur).wait()

    return run


def _expert_kernel(be_ref, nu_ref, tok_hbm, hp_hbm, wgu_ref, wd_ref, y_ref, idx_smem, xbuf0, xbuf1, xbuf2,
                   wgu_bf, wd_bf, sem_idx, sem_rows):
    b = pl.program_id(0)
    nu = nu_ref[0]
    run = _gather_pipeline(b, nu, tok_hbm, hp_hbm, idx_smem, (xbuf0, xbuf1, xbuf2), sem_idx, sem_rows, MOE_BLK)

    @pl.when((b == 0) | (be_ref[b] != be_ref[jnp.maximum(b - 1, 0)]))
    def _():
        rows = 256

        def cast_gu(i, carry):
            r = pl.multiple_of(i * rows, rows)
            wgu_bf[pl.ds(r, rows), :] = wgu_ref[pl.ds(r, rows), :].astype(BF16)
            return carry
        lax.fori_loop(0, D_MODEL // rows, cast_gu, 0)

        def cast_d(i, carry):
            r = pl.multiple_of(i * rows, rows)
            wd_bf[pl.ds(r, rows), :] = wd_ref[pl.ds(r, rows), :].astype(BF16)
            return carry
        lax.fori_loop(0, D_EXPERT // rows, cast_d, 0)

    def compute(xbuf):
        xa, xb = _unpack_halves(xbuf[...])
        gu = _dot(xa.astype(BF16), wgu_bf[:HALF, :]) + _dot(xb.astype(BF16), wgu_bf[HALF:, :])
        act = (_silu(gu[:, :D_EXPERT]) * gu[:, D_EXPERT:]).astype(BF16)
        y_ref[...] = _pack_halves(_dot(act, wd_bf[...]))

    run(compute)

    @pl.when(b >= nu)
    def _():
        y_ref[...] = jnp.zeros(y_ref.shape, U32)


def _experts(slot_tok, block_e, n_used, hp, w_gu, w_down, layer):
    grid_spec = pltpu.PrefetchScalarGridSpec(
        num_scalar_prefetch=2,
        grid=(N_BLOCKS,),
        in_specs=[
            pl.BlockSpec(memory_space=pl.ANY),
            pl.BlockSpec(memory_space=pl.ANY),
            pl.BlockSpec((None, None, D_MODEL, 2 * D_EXPERT), lambda b, be, nu: (layer, be[b], 0, 0)),
            pl.BlockSpec((None, None, D_EXPERT, D_MODEL), lambda b, be, nu: (layer, be[b], 0, 0)),
        ],
        out_specs=pl.BlockSpec((MOE_BLK, HALF), lambda b, be, nu: (b, 0)),
        scratch_shapes=[
            pltpu.SMEM((GATHER_IDX_ROWS, MOE_BLK), I32),
            pltpu.VMEM((MOE_BLK, HALF), U32),
            pltpu.VMEM((MOE_BLK, HALF), U32),
            pltpu.VMEM((MOE_BLK, HALF), U32),
            pltpu.VMEM((D_MODEL, 2 * D_EXPERT), BF16),
            pltpu.VMEM((D_EXPERT, D_MODEL), BF16),
            pltpu.SemaphoreType.DMA((GATHER_DEPTH,)),
            pltpu.SemaphoreType.DMA((GATHER_DEPTH,)),
        ],
    )
    return pl.pallas_call(
        _expert_kernel,
        grid_spec=grid_spec,
        out_shape=jax.ShapeDtypeStruct((N_SLOTS, HALF), U32),
        compiler_params=_params(("arbitrary",)),
        name="moe_experts",
    )(block_e, n_used, slot_tok, hp, w_gu, w_down)


COMB_ROWS = COMB_TM * TOP_K


def _combine_kernel(pos_hbm, y_hbm, x_ref, hp_ref, wt_ref, g2_ref, sgu_ref, sd_ref, fg_ref, o_ref,
                    idx_smem, ybuf0, ybuf1, ybuf2, sem_idx, sem_rows, *, final_norm):
    run = _gather_pipeline(pl.program_id(0), pl.num_programs(0), pos_hbm, y_hbm, idx_smem, (ybuf0, ybuf1, ybuf2),
                           sem_idx, sem_rows, COMB_ROWS)

    def compute(ybuf):
        ha, hb = _unpack_halves(hp_ref[...])
        gu = _dot(ha.astype(BF16), sgu_ref[:HALF, :]) + _dot(hb.astype(BF16), sgu_ref[HALF:, :])
        act = (_silu(gu[:, :D_SHARED]) * gu[:, D_SHARED:]).astype(BF16)
        ffn = _dot(act, sd_ref[...])
        acc_a = ffn[:, :HALF]
        acc_b = ffn[:, HALF:]
        for k in range(TOP_K):
            ya, yb = _unpack_halves(ybuf[k * COMB_TM:(k + 1) * COMB_TM, :])
            wk = wt_ref[:, k:k + 1]
            acc_a = acc_a + wk * ya
            acc_b = acc_b + wk * yb
        xa = x_ref[:, :HALF] + g2_ref[:, :HALF] * acc_a
        xb = x_ref[:, HALF:] + g2_ref[:, HALF:] * acc_b
        if final_norm:
            ms = (jnp.sum(xa * xa, axis=-1, keepdims=True) + jnp.sum(xb * xb, axis=-1, keepdims=True)) / D_MODEL
            inv = lax.rsqrt(ms + EPS)
            xa = xa * inv * fg_ref[:, :HALF]
            xb = xb * inv * fg_ref[:, HALF:]
        o_ref[:, :HALF] = xa
        o_ref[:, HALF:] = xb

    run(compute)


def _combine(pos, y, x, hp, wts, g2, sh_gu, sh_down, final_g, *, final_norm):
    n_tiles = N_TOK // COMB_TM
    pos_t = pos.reshape(n_tiles, COMB_TM, TOP_K).transpose(0, 2, 1).reshape(n_tiles, COMB_ROWS)
    kern = functools.partial(_combine_kernel, final_norm=final_norm)
    return pl.pallas_call(
        kern,
        grid=(n_tiles,),
        in_specs=[
            pl.BlockSpec(memory_space=pl.ANY),
            pl.BlockSpec(memory_space=pl.ANY),
            pl.BlockSpec((COMB_TM, D_MODEL), lambda i: (i, 0)),
            pl.BlockSpec((COMB_TM, HALF), lambda i: (i, 0)),
            pl.BlockSpec((COMB_TM, TOP_K), lambda i: (i, 0)),
            pl.BlockSpec((None, 1, D_MODEL), lambda i: (_mod_row(i * COMB_TM), 0, 0)),
            pl.BlockSpec((D_MODEL, 2 * D_SHARED), lambda i: (0, 0)),
            pl.BlockSpec((D_SHARED, D_MODEL), lambda i: (0, 0)),
            pl.BlockSpec((1, D_MODEL), lambda i: (0, 0)),
        ],
        out_specs=pl.BlockSpec((COMB_TM, D_MODEL), lambda i: (i, 0)),
        out_shape=jax.ShapeDtypeStruct((N_TOK, D_MODEL), F32),
        scratch_shapes=[
            pltpu.SMEM((GATHER_IDX_ROWS, COMB_ROWS), I32),
            pltpu.VMEM((COMB_ROWS, HALF), U32),
            pltpu.VMEM((COMB_ROWS, HALF), U32),
            pltpu.VMEM((COMB_ROWS, HALF), U32),
            pltpu.SemaphoreType.DMA((GATHER_DEPTH,)),
            pltpu.SemaphoreType.DMA((GATHER_DEPTH,)),
        ],
        compiler_params=_params(("arbitrary",)),
        name="moe_combine",
    )(pos_t, y, x, hp, wts, g2, sh_gu, sh_down, final_g)


def _moe(x, g, sc, sh, g2, router_w, router_b, w_gu, w_down, layer, sh_gu, sh_down, final_g, *, final_norm):
    hp, idx_t, wt_t = _moe_pre(x, g, sc, sh, router_w.T, router_b.reshape(N_EXPERTS, 1))
    slot_tok, pos, block_e, n_used = _route(idx_t)
    y = _experts(slot_tok, block_e, n_used, hp, w_gu, w_down, layer)
    return _combine(pos, y, x, hp, wt_t.T, g2, sh_gu.astype(BF16), sh_down.astype(BF16),
                    final_g, final_norm=final_norm)


def kernel(x_prompt, x_sample, state_dn, c, c_ctx, ada_w, ada_b, norm_g, final_g, dn_w_in, dn_conv, dn_a_log,
           dn_dt_bias, dn_onorm_g, dn_w_out, pool_w, pool_scale, moe_router, moe_bias, moe_w_gu, moe_w_down,
           sh_w_gu, sh_w_down):
    x = jnp.concatenate([x_prompt.reshape(NP_TOK, D_MODEL), x_sample.reshape(NS_TOK, D_MODEL)], axis=0)
    cvec = jnp.concatenate([c_ctx[None, :], c, jnp.zeros((N_MOD - 1 - DEC_BATCH, D_MODEL), F32)], axis=0)
    mod = _ada_mod(cvec, ada_w, ada_b)
    mod = mod.reshape(mod.shape[0], N_MOD, 6, 1, D_MODEL)
    fg = final_g.reshape(1, D_MODEL)

    def mods(layer):
        return [mod[layer, :, p] for p in range(6)]

    sh1, sc1, g1, sh2, sc2, g2 = mods(0)
    o, proj, new_state = _deltanet_core(x, norm_g[0, 0].reshape(1, D_MODEL), sc1, sh1, dn_w_in[0], dn_conv[0],
                                        dn_a_log[0], dn_dt_bias[0], state_dn)
    x = _dn_out(o, proj, dn_onorm_g[0].reshape(1, DN_DV), dn_w_out[0].astype(BF16), x, g1)
    x = _moe(x, norm_g[0, 1].reshape(1, D_MODEL), sc2, sh2, g2, moe_router[0], moe_bias[0], moe_w_gu,
             moe_w_down, 0, sh_w_gu[0], sh_w_down[0], fg, final_norm=False)

    sh1, sc1, g1, sh2, sc2, g2 = mods(1)
    x = _pool_mixer(x, _rstd(x), norm_g[1, 0].reshape(1, D_MODEL), sc1, sh1, pool_w[0].astype(BF16),
                    pool_scale[0].reshape(1, D_MODEL), g1)
    y = _moe(x, norm_g[1, 1].reshape(1, D_MODEL), sc2, sh2, g2, moe_router[1], moe_bias[1], moe_w_gu,
             moe_w_down, 1, sh_w_gu[1], sh_w_down[1], fg, final_norm=True)

    y_prompt = y[:NP_TOK].reshape(BATCH, SEQ, D_MODEL)
    y_sample = y[NP_TOK:].reshape(DEC_BATCH, DEC_SEQ, D_MODEL)
    return (y_prompt, y_sample, new_state)
```

```python
import functools
import math

import jax
import jax.numpy as jnp
from jax import lax
from jax.experimental import pallas as pl
from jax.experimental.pallas import tpu as pltpu

F32 = jnp.float32
BF16 = jnp.bfloat16
I32 = jnp.int32
U32 = jnp.uint32

D_MODEL = 2048
BATCH = 16
SEQ = 256
DEC_BATCH = 8
DEC_SEQ = 2048
GRID_W = 64
NP_TOK = BATCH * SEQ
NS_TOK = DEC_BATCH * DEC_SEQ
N_TOK = NP_TOK + NS_TOK
N_MOD = 16

DN_DK = 128
DN_DV = 128
DN_NK = D_MODEL // DN_DK
DN_NV = 2 * DN_NK
DN_QK = DN_NK * DN_DK
DN_V = DN_NV * DN_DV
DN_CONV = 5
CHUNK = 64
DN_HKB = 4
N_GATE = 4 * DN_NV

POOL_WINDOWS = (2, 4, 8, 16)
POOL_GROUP = D_MODEL // len(POOL_WINDOWS)

N_EXPERTS = 64
TOP_K = 8
N_EXPERT_GROUPS = 8
GROUP_SIZE = N_EXPERTS // N_EXPERT_GROUPS
TOPK_GROUPS = 4
D_EXPERT = 512
D_SHARED = 512
ROUTED_SCALE = 2.5
EPS = 1e-6

MOE_BLK = 256
N_ASSIGN = N_TOK * TOP_K
N_BLOCKS = (N_ASSIGN + N_EXPERTS * (MOE_BLK - 1) + MOE_BLK - 1) // MOE_BLK
N_SLOTS = N_BLOCKS * MOE_BLK
HALF = D_MODEL // 2
COMB_TM = 128
VMEM_LIMIT = 56 * 1024 * 1024

_NT = (((1,), (1,)), ((), ()))
_TN = (((0,), (0,)), ((), ()))


def _nt(a, b, precision=None):
    return lax.dot_general(a, b, _NT, preferred_element_type=F32, precision=precision)


def _tn(a, b):
    return lax.dot_general(a, b, _TN, preferred_element_type=F32)


def _dot(a, b, precision=None):
    return jnp.dot(a, b, preferred_element_type=F32, precision=precision)


def _silu(x):
    return x * jax.nn.sigmoid(x)


def _mod_row(tile_start):
    return jnp.where(tile_start < NP_TOK, 0, 1 + (tile_start - NP_TOK) // DEC_SEQ)


def _params(sem, vmem=VMEM_LIMIT):
    return pltpu.CompilerParams(dimension_semantics=sem, vmem_limit_bytes=vmem)


def _pack_halves(y):
    a = lax.bitcast_convert_type(y[:, :HALF].astype(BF16).astype(F32), U32)
    b = lax.bitcast_convert_type(y[:, HALF:].astype(BF16).astype(F32), U32)
    return a | (b >> 16)


def _unpack_halves(u):
    a = lax.bitcast_convert_type(u & jnp.uint32(0xFFFF0000), F32)
    b = lax.bitcast_convert_type(u << 16, F32)
    return a, b


def _ada_kernel(c_ref, w_ref, b_ref, o_ref):
    a = _silu(c_ref[...]).astype(BF16)
    o_ref[...] = _dot(a, w_ref[...].astype(BF16)) + b_ref[...]


def _ada_mod(cvec, ada_w, ada_b):
    depth, d, n = ada_w.shape
    tn = 1024
    return pl.pallas_call(
        _ada_kernel,
        grid=(depth, n // tn),
        in_specs=[
            pl.BlockSpec((N_MOD, d), lambda l, j: (0, 0)),
            pl.BlockSpec((None, d, tn), lambda l, j: (l, 0, j)),
            pl.BlockSpec((None, 1, tn), lambda l, j: (l, 0, j)),
        ],
        out_specs=pl.BlockSpec((None, N_MOD, tn), lambda l, j: (l, 0, j)),
        out_shape=jax.ShapeDtypeStruct((depth, N_MOD, n), F32),
        compiler_params=_params(("parallel", "parallel")),
        name="ada_mod",
    )(cvec, ada_w, ada_b.reshape(depth, 1, n))


def _norm_mod(x, g, sc, sh):
    ms = jnp.mean(x * x, axis=-1, keepdims=True)
    return x * lax.rsqrt(ms + EPS) * g * (1.0 + sc) + sh


def _in_proj_kernel(x_ref, g_ref, sc_ref, sh_ref, w_ref, wab_ref, wab_lo_ref, o_ref, ab_ref, h_scr):
    @pl.when(pl.program_id(1) == 0)
    def _():
        hf = _norm_mod(x_ref[...], g_ref[...], sc_ref[...], sh_ref[...])
        h = hf.astype(BF16)
        h_scr[...] = h
        h_lo = (hf - h.astype(F32)).astype(BF16)
        ab_ref[...] = _dot(h, wab_ref[...]) + (_dot(h_lo, wab_ref[...]) + _dot(h, wab_lo_ref[...]))

    o_ref[...] = _dot(h_scr[...], w_ref[...]).astype(o_ref.dtype)


def _in_proj(x, g, sc, sh, w, wab_f32):
    wab = wab_f32.astype(BF16)
    wab_lo = (wab_f32 - wab.astype(F32)).astype(BF16)
    tm, tn = 1024, 1024
    n = w.shape[1]
    mod_spec = pl.BlockSpec((None, 1, D_MODEL), lambda i, j: (_mod_row(i * tm), 0, 0))
    return pl.pallas_call(
        _in_proj_kernel,
        grid=(N_TOK // tm, n // tn),
        in_specs=[
            pl.BlockSpec((tm, D_MODEL), lambda i, j: (i, 0)),
            pl.BlockSpec((1, D_MODEL), lambda i, j: (0, 0)),
            mod_spec,
            mod_spec,
            pl.BlockSpec((D_MODEL, tn), lambda i, j: (0, j)),
            pl.BlockSpec((D_MODEL, N_GATE), lambda i, j: (0, 0)),
            pl.BlockSpec((D_MODEL, N_GATE), lambda i, j: (0, 0)),
        ],
        out_specs=[
            pl.BlockSpec((tm, tn), lambda i, j: (i, j)),
            pl.BlockSpec((tm, N_GATE), lambda i, j: (i, 0)),
        ],
        out_shape=[
            jax.ShapeDtypeStruct((N_TOK, n), BF16),
            jax.ShapeDtypeStruct((N_TOK, N_GATE), F32),
        ],
        scratch_shapes=[pltpu.VMEM((tm, D_MODEL), BF16)],
        compiler_params=_params(("parallel", "arbitrary")),
        name="dn_in_proj",
    )(x, g, sc, sh, w, wab, wab_lo)


def _gates_kernel(ab_ref, alog_ref, dt_ref, o_ref):
    tm = ab_ref.shape[0]
    a = ab_ref[...]
    z = a + dt_ref[...]
    softplus = jnp.maximum(z, 0.0) + jnp.log(1.0 + jnp.exp(-jnp.abs(z)))
    g = -jnp.exp(alog_ref[...]) * softplus
    beta = jax.nn.sigmoid(a)
    ii = lax.broadcasted_iota(I32, (tm, tm), 0)
    jj = lax.broadcasted_iota(I32, (tm, tm), 1)
    same = (ii // CHUNK) == (jj // CHUNK)
    m_f = jnp.where(same & (jj <= ii), 1.0, 0.0).astype(F32)
    m_b = jnp.where(same & (jj >= ii), 1.0, 0.0).astype(F32)
    gc_f = _dot(m_f, g, precision=lax.Precision.HIGHEST)
    gc_b = _dot(m_b, g, precision=lax.Precision.HIGHEST)
    lane = lax.broadcasted_iota(I32, a.shape, 1)
    is_g = (lane % (2 * DN_NV)) < DN_NV
    o_ref[...] = jnp.where(is_g, jnp.where(lane < 2 * DN_NV, gc_f, gc_b), beta)


def _gates(ab, a_log, dt_bias):
    tm = 256
    zeros = jnp.zeros((DN_NV,), F32)
    alog = jnp.concatenate([a_log[0], zeros, a_log[1], zeros]).reshape(1, N_GATE)
    dt = jnp.concatenate([dt_bias[0], zeros, dt_bias[1], zeros]).reshape(1, N_GATE)
    return pl.pallas_call(
        _gates_kernel,
        grid=(N_TOK // tm,),
        in_specs=[
            pl.BlockSpec((tm, N_GATE), lambda i: (i, 0)),
            pl.BlockSpec((1, N_GATE), lambda i: (0, 0)),
            pl.BlockSpec((1, N_GATE), lambda i: (0, 0)),
        ],
        out_specs=pl.BlockSpec((tm, N_GATE), lambda i: (i, 0)),
        out_shape=jax.ShapeDtypeStruct((N_TOK, N_GATE), F32),
        compiler_params=_params(("parallel",)),
        name="dn_gates",
    )(ab, alog, dt)


CONV_ROWS = 2048
CONV_TC = 512


CONV_CHUNK = 128
CONV_EDGE = 16


def _conv_kernel(x_ref, w_ref, o_ref):
    i = pl.program_id(0)
    j = pl.program_id(1)
    rows = x_ref.shape[0]
    seq_len = jnp.where(i * rows < NP_TOK, SEQ, DEC_SEQ)
    half = DN_CONV // 2
    taps = [s for s in range(-half, half + 1) if s != 0]
    win_rows = CONV_CHUNK + 2 * CONV_EDGE
    q = lax.broadcasted_iota(I32, (len(taps) * CONV_CHUNK, win_rows), 0)
    col = lax.broadcasted_iota(I32, (len(taps) * CONV_CHUNK, win_rows), 1)
    src = CONV_EDGE + (q % CONV_CHUNK)
    for t, s in enumerate(taps):
        src = jnp.where(q // CONV_CHUNK == t, src + s, src)
    shift_mat = jnp.where(col == src, 1.0, 0.0).astype(BF16)
    is_qk = j < (2 * DN_QK) // CONV_TC
    qscale = jnp.where(j < DN_QK // CONV_TC, DN_DK ** -0.5, 1.0)

    def chunk(c, carry):
        r0 = pl.multiple_of(c * CONV_CHUNK, CONV_CHUNK)
        xc = x_ref[pl.ds(r0, CONV_CHUNK), :]
        r_prev = pl.multiple_of(jnp.maximum(r0 - CONV_EDGE, 0), CONV_EDGE)
        r_next = pl.multiple_of(jnp.minimum(r0 + CONV_CHUNK, rows - CONV_EDGE), CONV_EDGE)
        prev = x_ref[pl.ds(r_prev, CONV_EDGE), :]
        nxt = x_ref[pl.ds(r_next, CONV_EDGE), :]
        prev = jnp.where((r0 & (seq_len - 1)) != 0, prev, jnp.zeros_like(prev))
        nxt = jnp.where(((r0 + CONV_CHUNK) & (seq_len - 1)) != 0, nxt, jnp.zeros_like(nxt))
        shifted = _dot(shift_mat, jnp.concatenate([prev, xc, nxt], axis=0))
        acc = xc.astype(F32) * w_ref[half:half + 1, :]
        for t, s in enumerate(taps):
            acc = acc + shifted[t * CONV_CHUNK:(t + 1) * CONV_CHUNK] * w_ref[half + s:half + s + 1, :]
        y = _silu(acc)
        for h in range(CONV_TC // DN_DK):
            sl = slice(h * DN_DK, (h + 1) * DN_DK)
            yh = y[:, sl]
            inv = lax.rsqrt(jnp.sum(yh * yh, axis=-1, keepdims=True) + EPS) * qscale
            o_ref[pl.ds(r0, CONV_CHUNK), sl] = (yh * jnp.where(is_qk, inv, 1.0)).astype(o_ref.dtype)
        return carry

    lax.fori_loop(0, rows // CONV_CHUNK, chunk, 0, unroll=4)


def _conv_qkv(proj, conv_w):
    nch = 2 * DN_QK + DN_V
    return pl.pallas_call(
        _conv_kernel,
        grid=(N_TOK // CONV_ROWS, nch // CONV_TC),
        in_specs=[
            pl.BlockSpec((CONV_ROWS, CONV_TC), lambda i, j: (i, j)),
            pl.BlockSpec((DN_CONV, CONV_TC), lambda i, j: (0, j)),
        ],
        out_specs=pl.BlockSpec((CONV_ROWS, CONV_TC), lambda i, j: (i, j)),
        out_shape=jax.ShapeDtypeStruct((N_TOK, nch), BF16),
        compiler_params=_params(("parallel", "parallel")),
        name="dn_conv",
    )(proj, conv_w)


def _delta_kernel(*refs, n_chunks, has_s0, write_state):
    q_ref, k_ref, v_ref, gc_ref, rows_ref = refs[:5]
    pos = 5
    if has_s0:
        s0_ref = refs[pos]
        pos += 1
    o_ref = refs[pos]
    pos += 1
    if write_state:
        sfin_ref = refs[pos]
        pos += 1
    s_scr = refs[pos]
    out_ref = o_ref
    o_ref = refs[pos + 1]

    hk0 = pl.program_id(1) * DN_HKB
    if has_s0:
        for hl in range(DN_HKB):
            for d in range(2):
                for r in range(2):
                    s_scr[4 * hl + 2 * d + r] = s0_ref[d, 2 * hl + r]
    else:
        s_scr[...] = jnp.zeros(s_scr.shape, F32)
    o_ref[...] = jnp.zeros(o_ref.shape, F32)

    ii = lax.broadcasted_iota(I32, (CHUNK, CHUNK), 0)
    jj = lax.broadcasted_iota(I32, (CHUNK, CHUNK), 1)
    eye = jnp.where(ii == jj, 1.0, 0.0).astype(F32)
    lane = lax.broadcasted_iota(I32, (CHUNK, N_GATE), 1)
    pair_masks = [((ii >> (l + 1)) == (jj >> (l + 1))) & ((ii >> l) != (jj >> l))
                  for l in range(int(math.log2(CHUNK)))]

    chain_ids = [(hl, d, r) for hl in range(DN_HKB) for d in range(2) for r in range(2)]

    def chunk_index(t, d):
        return t if d == 0 else n_chunks - 1 - t

    def local_begin(t):
        chains = []
        for hl in range(DN_HKB):
            for d in range(2):
                c = chunk_index(t, d)
                r0 = pl.multiple_of(c * CHUNK, CHUNK)
                k = k_ref[pl.ds(r0, CHUNK), hl * DN_DK:(hl + 1) * DN_DK]
                q = q_ref[pl.ds(r0, CHUNK), hl * DN_DK:(hl + 1) * DN_DK]
                gates = gc_ref[pl.ds(r0, CHUNK), :]
                rows = rows_ref[hl, c]
                kk = _nt(k, k)
                qk = _nt(q, k)
                incl = (jj <= ii) if d == 0 else (jj >= ii)
                strict = (jj < ii) if d == 0 else (jj > ii)
                last = CHUNK - 1 if d == 0 else 0
                for r in range(2):
                    lane_g = d * 2 * DN_NV + 2 * (hk0 + hl) + r
                    gcc = jnp.sum(jnp.where(lane == lane_g, gates, 0.0), axis=1, keepdims=True)
                    bc = jnp.sum(jnp.where(lane == lane_g + DN_NV, gates, 0.0), axis=1, keepdims=True)
                    gcr = rows[2 * d + r:2 * d + r + 1, :]
                    gl = gcr[:, last:last + 1]
                    dec = jnp.where(incl, jnp.exp(jnp.where(incl, gcc - gcr, 0.0)), 0.0)
                    lm = jnp.where(strict, bc * kk * dec, 0.0)
                    v = v_ref[pl.ds(r0, CHUNK), (2 * hl + r) * DN_DV:(2 * hl + r + 1) * DN_DV]
                    chains.append(dict(k=k, q=q, v=v, gcc=gcc, bc=bc, gl=gl, attn=(qk * dec).astype(BF16),
                                       lb=lm.astype(BF16), tm=eye - jnp.where(pair_masks[0], lm, 0.0)))
        return chains

    def local_level(chains, lvl):
        for ch in chains:
            ch["tb"] = ch["tm"].astype(BF16)
            ch["bt"] = _dot(jnp.where(pair_masks[lvl], ch["lb"], jnp.zeros_like(ch["lb"])), ch["tb"])
        for ch in chains:
            ch["tm"] = ch["tm"] - _dot(ch["tb"], ch["bt"].astype(BF16))

    def local_finish(chains):
        for ci, ch in enumerate(chains):
            kf = ch["k"].astype(F32)
            egc = jnp.exp(ch["gcc"])
            rhs = jnp.concatenate([(ch["v"].astype(F32) * ch["bc"]).astype(BF16),
                                   (kf * (ch["bc"] * egc)).astype(BF16)], axis=1)
            ch["uw"] = _dot(ch["tm"].astype(BF16), rhs)
            ch["qg"] = (ch["q"].astype(F32) * egc).astype(BF16)
            ch["kd"] = (kf * jnp.exp(ch["gl"] - ch["gcc"])).astype(BF16)

    def scan_state_products(chains):
        for ci, ch in enumerate(chains):
            ch["s"] = s_scr[ci]
            wq = jnp.concatenate([ch["uw"][:, DN_DV:].astype(BF16), ch["qg"]], axis=0)
            ch["ws_qs"] = _dot(wq, ch["s"].astype(BF16))

    def scan_update(t, chains):
        for ci, (hl, d, r) in enumerate(chain_ids):
            ch = chains[ci]
            r0 = pl.multiple_of(chunk_index(t, d) * CHUNK, CHUNK)
            v_new = (ch["uw"][:, :DN_DV] - ch["ws_qs"][:CHUNK]).astype(BF16)
            o = ch["ws_qs"][CHUNK:] + _dot(ch["attn"], v_new)
            s_scr[ci] = ch["s"] * jnp.exp(ch["gl"]) + _tn(ch["kd"], v_new)
            oc = 2 * hl + r
            o_ref[pl.ds(r0, CHUNK), oc * DN_DV:(oc + 1) * DN_DV] += o

    def body(t, carry):
        chains = local_begin(t)
        for lvl in range(1, len(pair_masks)):
            local_level(chains, lvl)
        local_finish(chains)
        scan_state_products(chains)
        scan_update(t, chains)
        return carry

    lax.fori_loop(0, n_chunks, body, 0)

    def emit(c, carry):
        r0 = pl.multiple_of(c * CHUNK, CHUNK)
        out_ref[pl.ds(r0, CHUNK), :] = o_ref[pl.ds(r0, CHUNK), :].astype(out_ref.dtype)
        return carry

    lax.fori_loop(0, n_chunks, emit, 0)
    if write_state:
        for hl in range(DN_HKB):
            for d in range(2):
                for r in range(2):
                    sfin_ref[d, 2 * hl + r] = s_scr[4 * hl + 2 * d + r]


def _delta(qkv, gc, rows, s0, *, n_seq, seq_len, row_blk0, write_state):
    n_chunks = seq_len // CHUNK
    has_s0 = s0 is not None
    kern = functools.partial(_delta_kernel, n_chunks=n_chunks, has_s0=has_s0, write_state=write_state)
    ngrp = DN_NK // DN_HKB
    qk_w = DN_HKB * DN_DK
    v_w = 2 * DN_HKB * DN_DV
    in_specs = [
        pl.BlockSpec((seq_len, qk_w), lambda b, h: (row_blk0 + b, h)),
        pl.BlockSpec((seq_len, qk_w), lambda b, h: (row_blk0 + b, ngrp + h)),
        pl.BlockSpec((seq_len, v_w), lambda b, h: (row_blk0 + b, ngrp + h)),
        pl.BlockSpec((seq_len, N_GATE), lambda b, h: (row_blk0 + b, 0)),
        pl.BlockSpec((None, DN_HKB, n_chunks, 8, CHUNK), lambda b, h: (b, h, 0, 0, 0)),
    ]
    args = [qkv, qkv, qkv, gc, rows]
    state_spec = pl.BlockSpec((None, None, 2, 2 * DN_HKB, DN_DK, DN_DV), lambda b, h: (b, 0, 0, h, 0, 0))
    if has_s0:
        in_specs.append(state_spec)
        args.append(s0)
    out_specs = [pl.BlockSpec((seq_len, v_w), lambda b, h: (b, h))]
    out_shape = [jax.ShapeDtypeStruct((n_seq * seq_len, DN_V), BF16)]
    if write_state:
        out_specs.append(state_spec)
        out_shape.append(jax.ShapeDtypeStruct((n_seq, 1, 2, DN_NV, DN_DK, DN_DV), F32))
    return pl.pallas_call(
        kern,
        grid=(n_seq, ngrp),
        in_specs=in_specs,
        out_specs=out_specs,
        out_shape=out_shape,
        scratch_shapes=[pltpu.VMEM((4 * DN_HKB, DN_DK, DN_DV), F32), pltpu.VMEM((seq_len, v_w), F32)],
        compiler_params=_params(("parallel", "parallel")),
        name="dn_delta_%d" % seq_len,
    )(*args)


def _gate_rows(gc, n_seq, seq_len):
    n_chunks = seq_len // CHUNK
    g = gc.reshape(n_seq, n_chunks, CHUNK, 2, 2, DN_NK, 2)
    g = g.transpose(0, 5, 1, 4, 3, 6, 2)
    return g.reshape(n_seq, DN_NK, n_chunks, 8, CHUNK)


def _deltanet_core(x, g, sc1, sh1, w_in, conv_w, a_log, dt_bias, state_dn, sample=True):
    nqkv = 2 * DN_QK + 2 * DN_V
    proj, ab = _in_proj(x, g, sc1, sh1, w_in[:, :nqkv].astype(BF16), w_in[:, nqkv:])
    gc = _gates(ab, a_log, dt_bias)
    qkv = _conv_qkv(proj, conv_w)
    o_p, new_state = _delta(qkv, gc, _gate_rows(gc[:NP_TOK], BATCH, SEQ), None,
                            n_seq=BATCH, seq_len=SEQ, row_blk0=0, write_state=True)
    if not sample:
        return o_p, proj, new_state
    (o_s,) = _delta(qkv, gc, _gate_rows(gc[NP_TOK:], DEC_BATCH, DEC_SEQ), state_dn,
                    n_seq=DEC_BATCH, seq_len=DEC_SEQ, row_blk0=NP_TOK // DEC_SEQ, write_state=False)
    return jnp.concatenate([o_p, o_s], axis=0), proj, new_state


def _dn_out_kernel(o_ref, z_ref, og_ref, w_ref, x_ref, g1_ref, out_ref, a_scr):
    @pl.when(pl.program_id(1) == 0)
    def _():
        for h in range(DN_NV):
            sl = slice(h * DN_DV, (h + 1) * DN_DV)
            o = o_ref[:, sl].astype(F32)
            z = z_ref[:, sl].astype(F32)
            on = o * lax.rsqrt(jnp.mean(o * o, axis=-1, keepdims=True) + EPS) * og_ref[...]
            a_scr[:, sl] = (on * _silu(z)).astype(BF16)

    out_ref[...] = x_ref[...] + g1_ref[...] * _dot(a_scr[...], w_ref[...])


def _dn_out(o, proj, onorm_g, w_out, x, g1):
    tm, tn = 256, D_MODEL
    zblk = (2 * DN_QK + DN_V) // DN_V
    return pl.pallas_call(
        _dn_out_kernel,
        grid=(N_TOK // tm, D_MODEL // tn),
        in_specs=[
            pl.BlockSpec((tm, DN_V), lambda i, j: (i, 0)),
            pl.BlockSpec((tm, DN_V), lambda i, j: (i, zblk)),
            pl.BlockSpec((1, DN_DV), lambda i, j: (0, 0)),
            pl.BlockSpec((DN_V, tn), lambda i, j: (0, j), pipeline_mode=pl.Buffered(1)),
            pl.BlockSpec((tm, tn), lambda i, j: (i, j)),
            pl.BlockSpec((None, 1, tn), lambda i, j: (_mod_row(i * tm), 0, j)),
        ],
        out_specs=pl.BlockSpec((tm, tn), lambda i, j: (i, j)),
        out_shape=jax.ShapeDtypeStruct((N_TOK, D_MODEL), F32),
        scratch_shapes=[pltpu.VMEM((tm, DN_V), BF16)],
        compiler_params=_params(("parallel", "arbitrary")),
        name="dn_out",
    )(o, proj, onorm_g, w_out, x, g1)


def _rstd_kernel(x_ref, o_ref):
    x = x_ref[...]
    o_ref[...] = lax.rsqrt(jnp.mean(x * x, axis=-1, keepdims=True) + EPS)


def _rstd(x):
    tm = 1024
    return pl.pallas_call(
        _rstd_kernel,
        grid=(N_TOK // tm,),
        in_specs=[pl.BlockSpec((tm, D_MODEL), lambda i: (i, 0))],
        out_specs=pl.BlockSpec((tm, 1), lambda i: (i, 0)),
        out_shape=jax.ShapeDtypeStruct((N_TOK, 1), F32),
        compiler_params=_params(("parallel",)),
        name="rstd",
    )(x)


POOL_ROWS = 2048


def _window_sum(x, pos, length, stride, w):
    rows = x.shape[0]
    m = w // 2

    def shifted(a, off):
        valid = (pos + off >= 0) & (pos + off < length)
        return jnp.where(valid, pltpu.roll(a, (-off * stride) % rows, 0), 0.0)

    fwd = x
    bwd = x
    step = 1
    while step < m:
        fwd = fwd + shifted(fwd, step)
        bwd = bwd + shifted(bwd, -step)
        step *= 2
    return fwd + shifted(bwd, -1)


def _window_count(pos, length, w):
    lo = jnp.maximum(pos - w // 2, 0)
    hi = jnp.minimum(pos + (w - w // 2) - 1, length - 1)
    return (hi - lo + 1).astype(F32)


def _pool_kernel(x_ref, rstd_ref, g_ref, sc_ref, sh_ref, w_ref, ps_ref, g1_ref, o_ref):
    i = pl.program_id(0)
    grp = pl.program_id(1)
    x = x_ref[...]
    h = x * rstd_ref[...] * g_ref[...] * (1.0 + sc_ref[...]) + sh_ref[...]
    ridx = lax.broadcasted_iota(I32, (POOL_ROWS, 1), 0)

    def finish(pooled):
        p = (pooled - h).astype(BF16)
        y = _dot(p, w_ref[...]) * ps_ref[...]
        o_ref[...] = x + g1_ref[...] * y

    for gi, w in enumerate(POOL_WINDOWS):
        @pl.when((grp == gi) & (i * POOL_ROWS < NP_TOK))
        def _(w=w):
            pos = ridx & (SEQ - 1)
            s = _window_sum(h, pos, SEQ, 1, w)
            finish(s / _window_count(pos, SEQ, w))

        @pl.when((grp == gi) & (i * POOL_ROWS >= NP_TOK))
        def _(w=w):
            col = ridx & (GRID_W - 1)
            row = ridx // GRID_W
            n_rows = POOL_ROWS // GRID_W
            s = _window_sum(h, col, GRID_W, 1, w)
            s = _window_sum(s, row, n_rows, GRID_W, w)
            finish(s / (_window_count(col, GRID_W, w) * _window_count(row, n_rows, w)))


def _pool_mixer(x, rstd, g, sc, sh, pool_w, pool_scale, g1):
    ngrp = len(POOL_WINDOWS)
    mod = lambda i, j: (_mod_row(i * POOL_ROWS), 0, j)
    return pl.pallas_call(
        _pool_kernel,
        grid=(N_TOK // POOL_ROWS, ngrp),
        in_specs=[
            pl.BlockSpec((POOL_ROWS, POOL_GROUP), lambda i, j: (i, j)),
            pl.BlockSpec((POOL_ROWS, 1), lambda i, j: (i, 0)),
            pl.BlockSpec((1, POOL_GROUP), lambda i, j: (0, j)),
            pl.BlockSpec((None, 1, POOL_GROUP), mod),
            pl.BlockSpec((None, 1, POOL_GROUP), mod),
            pl.BlockSpec((None, POOL_GROUP, POOL_GROUP), lambda i, j: (j, 0, 0)),
            pl.BlockSpec((1, POOL_GROUP), lambda i, j: (0, j)),
            pl.BlockSpec((None, 1, POOL_GROUP), mod),
        ],
        out_specs=pl.BlockSpec((POOL_ROWS, POOL_GROUP), lambda i, j: (i, j)),
        out_shape=jax.ShapeDtypeStruct((N_TOK, D_MODEL), F32),
        compiler_params=_params(("parallel", "parallel")),
        name="pool_mixer",
    )(x, rstd, g, sc, sh, pool_w, pool_scale, g1)


def _moe_pre_kernel(x_ref, g_ref, sc_ref, sh_ref, rw_ref, rb_ref, hp_ref, idx_ref, wt_ref):
    tm = x_ref.shape[0]
    h = _norm_mod(x_ref[...], g_ref[...], sc_ref[...], sh_ref[...])
    hp_ref[...] = _pack_halves(h)
    logits = _nt(rw_ref[...], h, precision=lax.Precision.HIGHEST)
    s = jax.nn.sigmoid(logits)
    choice = s + rb_ref[...]
    neg = -jnp.inf
    row8 = lax.broadcasted_iota(I32, (GROUP_SIZE, tm), 0)
    rowg = lax.broadcasted_iota(I32, (N_EXPERT_GROUPS, tm), 0)

    def first_max(vals, riota, n):
        m = jnp.max(vals, axis=0, keepdims=True)
        i = jnp.min(jnp.where(vals == m, riota, n), axis=0, keepdims=True)
        return m, i

    gscore = jnp.zeros((N_EXPERT_GROUPS, tm), F32)
    for g in range(N_EXPERT_GROUPS):
        cg = choice[g * GROUP_SIZE:(g + 1) * GROUP_SIZE, :]
        m1, i1 = first_max(cg, row8, GROUP_SIZE)
        m2 = jnp.max(jnp.where(row8 == i1, neg, cg), axis=0, keepdims=True)
        gscore = jnp.where(rowg == g, m1 + m2, gscore)
    gsel = jnp.zeros((N_EXPERT_GROUPS, tm), F32)
    for _ in range(TOPK_GROUPS):
        _, i = first_max(gscore, rowg, N_EXPERT_GROUPS)
        gsel = jnp.where(rowg == i, 1.0, gsel)
        gscore = jnp.where(rowg == i, neg, gscore)
    masked = jnp.concatenate(
        [jnp.where(gsel[g:g + 1, :] > 0.0, choice[g * GROUP_SIZE:(g + 1) * GROUP_SIZE, :], neg)
         for g in range(N_EXPERT_GROUPS)], axis=0)
    rowe = lax.broadcasted_iota(I32, (N_EXPERTS, tm), 0)
    rowk = lax.broadcasted_iota(I32, (TOP_K, tm), 0)
    idx = jnp.zeros((TOP_K, tm), I32)
    wts = jnp.zeros((TOP_K, tm), F32)
    for kk in range(TOP_K):
        _, i = first_max(masked, rowe, N_EXPERTS)
        sel = rowe == i
        wk = jnp.sum(jnp.where(sel, s, 0.0), axis=0, keepdims=True)
        idx = jnp.where(rowk == kk, i, idx)
        wts = jnp.where(rowk == kk, wk, wts)
        masked = jnp.where(sel, neg, masked)
    wts = wts / jnp.sum(wts, axis=0, keepdims=True) * ROUTED_SCALE
    idx_ref[...] = idx
    wt_ref[...] = wts


def _moe_pre(x, g, sc, sh, router_wt, router_b):
    tm = 256
    mod_spec = pl.BlockSpec((None, 1, D_MODEL), lambda i: (_mod_row(i * tm), 0, 0))
    return pl.pallas_call(
        _moe_pre_kernel,
        grid=(N_TOK // tm,),
        in_specs=[
            pl.BlockSpec((tm, D_MODEL), lambda i: (i, 0)),
            pl.BlockSpec((1, D_MODEL), lambda i: (0, 0)),
            mod_spec,
            mod_spec,
            pl.BlockSpec((N_EXPERTS, D_MODEL), lambda i: (0, 0)),
            pl.BlockSpec((N_EXPERTS, 1), lambda i: (0, 0)),
        ],
        out_specs=[
            pl.BlockSpec((tm, HALF), lambda i: (i, 0)),
            pl.BlockSpec((TOP_K, tm), lambda i: (0, i)),
            pl.BlockSpec((TOP_K, tm), lambda i: (0, i)),
        ],
        out_shape=[
            jax.ShapeDtypeStruct((N_TOK, HALF), U32),
            jax.ShapeDtypeStruct((TOP_K, N_TOK), I32),
            jax.ShapeDtypeStruct((TOP_K, N_TOK), F32),
        ],
        compiler_params=_params(("parallel",)),
        name="moe_pre",
    )(x, g, sc, sh, router_wt, router_b)


def _route(idx_t):
    flat_e = idx_t.T.reshape(-1)
    iota = jnp.arange(N_ASSIGN, dtype=I32)
    e_sorted, order = lax.sort_key_val(flat_e, iota)
    _, inv_order = lax.sort_key_val(order, iota)
    start = jnp.searchsorted(e_sorted, jnp.arange(N_EXPERTS, dtype=I32), side="left", method="scan").astype(I32)
    counts = jnp.concatenate([start[1:], jnp.full((1,), N_ASSIGN, I32)]) - start
    padded = (counts + MOE_BLK - 1) // MOE_BLK * MOE_BLK
    pend = jnp.cumsum(padded)
    pstart = pend - padded
    shift = pstart - start
    dshift = shift - jnp.concatenate([jnp.zeros((1,), I32), shift[:-1]])
    steps = jnp.zeros((N_ASSIGN,), I32).at[start].add(dshift, mode="drop")
    dest_sorted = iota + jnp.cumsum(steps)
    pos = dest_sorted[inv_order]
    block_start = jnp.arange(N_BLOCKS, dtype=I32) * MOE_BLK
    block_e = jnp.minimum(jnp.sum(pend[None, :] <= block_start[:, None], axis=1), N_EXPERTS - 1).astype(I32)
    slot = block_start[:, None] + jnp.arange(MOE_BLK, dtype=I32)[None, :]
    within = slot - pstart[block_e][:, None]
    valid = (within < counts[block_e][:, None]) & (slot < pend[N_EXPERTS - 1])
    src = jnp.clip(slot - shift[block_e][:, None], 0, N_ASSIGN - 1)
    slot_tok = jnp.where(valid, order[src] // TOP_K, 0).astype(I32)
    n_used = (pend[N_EXPERTS - 1] // MOE_BLK).astype(I32).reshape(1)
    return slot_tok.reshape(N_BLOCKS, MOE_BLK), pos.astype(I32), block_e, n_used


GATHER_DEPTH = 3
GATHER_IDX_ROWS = 4


def _gather_pipeline(step, n_steps, idx_hbm, src_hbm, idx_smem, bufs, sem_idx, sem_rows, n_rows):
    last = idx_hbm.shape[0] - 1
    nbuf = GATHER_DEPTH

    def idx_copy(blk, slot):
        return pltpu.make_async_copy(idx_hbm.at[jnp.minimum(blk, last)], idx_smem.at[slot], sem_idx.at[slot])

    def row_copy(slot, r, row):
        return pltpu.make_async_copy(src_hbm.at[pl.ds(row, 1)], bufs[slot].at[pl.ds(r, 1)], sem_rows.at[slot])

    def wait_rows(slot):
        pltpu.make_async_copy(src_hbm.at[pl.ds(0, n_rows)], bufs[slot], sem_rows.at[slot]).wait()

    @pl.when(step == 0)
    def _():
        for s in range(nbuf - 1):
            idx_copy(s, s).start()
            idx_copy(s, s).wait()

            def one(r, carry, s=s):
                row_copy(s, r, idx_smem[s, r]).start(priority=s % 2)
                return carry
            lax.fori_loop(0, n_rows, one, 0, unroll=8)
        idx_copy(nbuf - 1, nbuf - 1).start()

    def run(compute):
        for cur in range(nbuf):
            new = (cur + nbuf - 1) % nbuf

            @pl.when((step < n_steps) & (step % nbuf == cur))
            def _(cur=cur, new=new):
                idx_copy(step + nbuf - 1, new).wait()
                wait_rows(cur)
                for r in range(n_rows):
                    row_copy(new, r, idx_smem[new, r]).start(priority=r % 2)
                idx_copy(step + nbuf, cur).start()
                compute(bufs[cur])

                @pl.when(step == n_steps - 1)
                def _():
                    for s in range(1, nbuf):
                        wait_rows((cur + s) % nbuf)
                    idx_copy(last, cur).wait()

    return run


def _expert_kernel(be_ref, nu_ref, tok_hbm, hp_hbm, wgu_ref, wd_ref, y_ref, idx_smem, xbuf0, xbuf1, xbuf2,
                   wgu_bf, wd_bf, sem_idx, sem_rows):
    b = pl.program_id(0)
    nu = nu_ref[0]
    run = _gather_pipeline(b, nu, tok_hbm, hp_hbm, idx_smem, (xbuf0, xbuf1, xbuf2), sem_idx, sem_rows, MOE_BLK)

    @pl.when((b == 0) | (be_ref[b] != be_ref[jnp.maximum(b - 1, 0)]))
    def _():
        rows = 256

        def cast_gu(i, carry):
            r = pl.multiple_of(i * rows, rows)
            wgu_bf[pl.ds(r, rows), :] = wgu_ref[pl.ds(r, rows), :].astype(BF16)
            return carry
        lax.fori_loop(0, D_MODEL // rows, cast_gu, 0)

        def cast_d(i, carry):
            r = pl.multiple_of(i * rows, rows)
            wd_bf[pl.ds(r, rows), :] = wd_ref[pl.ds(r, rows), :].astype(BF16)
            return carry
        lax.fori_loop(0, D_EXPERT // rows, cast_d, 0)

    def compute(xbuf):
        xa, xb = _unpack_halves(xbuf[...])
        gu = _dot(xa.astype(BF16), wgu_bf[:HALF, :]) + _dot(xb.astype(BF16), wgu_bf[HALF:, :])
        act = (_silu(gu[:, :D_EXPERT]) * gu[:, D_EXPERT:]).astype(BF16)
        y_ref[...] = _pack_halves(_dot(act, wd_bf[...]))

    run(compute)

    @pl.when(b >= nu)
    def _():
        y_ref[...] = jnp.zeros(y_ref.shape, U32)


def _experts(slot_tok, block_e, n_used, hp, w_gu, w_down, layer):
    grid_spec = pltpu.PrefetchScalarGridSpec(
        num_scalar_prefetch=2,
        grid=(N_BLOCKS,),
        in_specs=[
            pl.BlockSpec(memory_space=pl.ANY),
            pl.BlockSpec(memory_space=pl.ANY),
            pl.BlockSpec((None, None, D_MODEL, 2 * D_EXPERT), lambda b, be, nu: (layer, be[b], 0, 0)),
            pl.BlockSpec((None, None, D_EXPERT, D_MODEL), lambda b, be, nu: (layer, be[b], 0, 0)),
        ],
        out_specs=pl.BlockSpec((MOE_BLK, HALF), lambda b, be, nu: (b, 0)),
        scratch_shapes=[
            pltpu.SMEM((GATHER_IDX_ROWS, MOE_BLK), I32),
            pltpu.VMEM((MOE_BLK, HALF), U32),
            pltpu.VMEM((MOE_BLK, HALF), U32),
            pltpu.VMEM((MOE_BLK, HALF), U32),
            pltpu.VMEM((D_MODEL, 2 * D_EXPERT), BF16),
            pltpu.VMEM((D_EXPERT, D_MODEL), BF16),
            pltpu.SemaphoreType.DMA((GATHER_DEPTH,)),
            pltpu.SemaphoreType.DMA((GATHER_DEPTH,)),
        ],
    )
    return pl.pallas_call(
        _expert_kernel,
        grid_spec=grid_spec,
        out_shape=jax.ShapeDtypeStruct((N_SLOTS, HALF), U32),
        compiler_params=_params(("arbitrary",)),
        name="moe_experts",
    )(block_e, n_used, slot_tok, hp, w_gu, w_down)


COMB_ROWS = COMB_TM * TOP_K


def _combine_kernel(pos_hbm, y_hbm, x_ref, hp_ref, wt_ref, g2_ref, sgu_ref, sd_ref, fg_ref, o_ref,
                    idx_smem, ybuf0, ybuf1, ybuf2, sem_idx, sem_rows, *, final_norm):
    run = _gather_pipeline(pl.program_id(0), pl.num_programs(0), pos_hbm, y_hbm, idx_smem, (ybuf0, ybuf1, ybuf2),
                           sem_idx, sem_rows, COMB_ROWS)

    def compute(ybuf):
        ha, hb = _unpack_halves(hp_ref[...])
        gu = _dot(ha.astype(BF16), sgu_ref[:HALF, :]) + _dot(hb.astype(BF16), sgu_ref[HALF:, :])
        act = (_silu(gu[:, :D_SHARED]) * gu[:, D_SHARED:]).astype(BF16)
        ffn = _dot(act, sd_ref[...])
        acc_a = ffn[:, :HALF]
        acc_b = ffn[:, HALF:]
        for k in range(TOP_K):
            ya, yb = _unpack_halves(ybuf[k * COMB_TM:(k + 1) * COMB_TM, :])
            wk = wt_ref[:, k:k + 1]
            acc_a = acc_a + wk * ya
            acc_b = acc_b + wk * yb
        xa = x_ref[:, :HALF] + g2_ref[:, :HALF] * acc_a
        xb = x_ref[:, HALF:] + g2_ref[:, HALF:] * acc_b
        if final_norm:
            ms = (jnp.sum(xa * xa, axis=-1, keepdims=True) + jnp.sum(xb * xb, axis=-1, keepdims=True)) / D_MODEL
            inv = lax.rsqrt(ms + EPS)
            xa = xa * inv * fg_ref[:, :HALF]
            xb = xb * inv * fg_ref[:, HALF:]
        o_ref[:, :HALF] = xa
        o_ref[:, HALF:] = xb

    run(compute)


def _combine(pos, y, x, hp, wts, g2, sh_gu, sh_down, final_g, *, final_norm):
    n_tiles = N_TOK // COMB_TM
    pos_t = pos.reshape(n_tiles, COMB_TM, TOP_K).transpose(0, 2, 1).reshape(n_tiles, COMB_ROWS)
    kern = functools.partial(_combine_kernel, final_norm=final_norm)
    return pl.pallas_call(
        kern,
        grid=(n_tiles,),
        in_specs=[
            pl.BlockSpec(memory_space=pl.ANY),
            pl.BlockSpec(memory_space=pl.ANY),
            pl.BlockSpec((COMB_TM, D_MODEL), lambda i: (i, 0)),
            pl.BlockSpec((COMB_TM, HALF), lambda i: (i, 0)),
            pl.BlockSpec((COMB_TM, TOP_K), lambda i: (i, 0)),
            pl.BlockSpec((None, 1, D_MODEL), lambda i: (_mod_row(i * COMB_TM), 0, 0)),
            pl.BlockSpec((D_MODEL, 2 * D_SHARED), lambda i: (0, 0)),
            pl.BlockSpec((D_SHARED, D_MODEL), lambda i: (0, 0)),
            pl.BlockSpec((1, D_MODEL), lambda i: (0, 0)),
        ],
        out_specs=pl.BlockSpec((COMB_TM, D_MODEL), lambda i: (i, 0)),
        out_shape=jax.ShapeDtypeStruct((N_TOK, D_MODEL), F32),
        scratch_shapes=[
            pltpu.SMEM((GATHER_IDX_ROWS, COMB_ROWS), I32),
            pltpu.VMEM((COMB_ROWS, HALF), U32),
            pltpu.VMEM((COMB_ROWS, HALF), U32),
            pltpu.VMEM((COMB_ROWS, HALF), U32),
            pltpu.SemaphoreType.DMA((GATHER_DEPTH,)),
            pltpu.SemaphoreType.DMA((GATHER_DEPTH,)),
        ],
        compiler_params=_params(("arbitrary",)),
        name="moe_combine",
    )(pos_t, y, x, hp, wts, g2, sh_gu, sh_down, final_g)


def _moe(x, g, sc, sh, g2, router_w, router_b, w_gu, w_down, layer, sh_gu, sh_down, final_g, *, final_norm):
    hp, idx_t, wt_t = _moe_pre(x, g, sc, sh, router_w.T, router_b.reshape(N_EXPERTS, 1))
    slot_tok, pos, block_e, n_used = _route(idx_t)
    y = _experts(slot_tok, block_e, n_used, hp, w_gu, w_down, layer)
    return _combine(pos, y, x, hp, wt_t.T, g2, sh_gu.astype(BF16), sh_down.astype(BF16),
                    final_g, final_norm=final_norm)


def kernel(x_prompt, x_sample, state_dn, c, c_ctx, ada_w, ada_b, norm_g, final_g, dn_w_in, dn_conv, dn_a_log,
           dn_dt_bias, dn_onorm_g, dn_w_out, pool_w, pool_scale, moe_router, moe_bias, moe_w_gu, moe_w_down,
           sh_w_gu, sh_w_down):
    x = jnp.concatenate([x_prompt.reshape(NP_TOK, D_MODEL), x_sample.reshape(NS_TOK, D_MODEL)], axis=0)
    cvec = jnp.concatenate([c_ctx[None, :], c, jnp.zeros((N_MOD - 1 - DEC_BATCH, D_MODEL), F32)], axis=0)
    mod = _ada_mod(cvec, ada_w, ada_b)
    mod = mod.reshape(mod.shape[0], N_MOD, 6, 1, D_MODEL)
    fg = final_g.reshape(1, D_MODEL)

    def mods(layer):
        return [mod[layer, :, p] for p in range(6)]

    sh1, sc1, g1, sh2, sc2, g2 = mods(0)
    o, proj, new_state = _deltanet_core(x, norm_g[0, 0].reshape(1, D_MODEL), sc1, sh1, dn_w_in[0], dn_conv[0],
                                        dn_a_log[0], dn_dt_bias[0], state_dn)
    x = _dn_out(o, proj, dn_onorm_g[0].reshape(1, DN_DV), dn_w_out[0].astype(BF16), x, g1)
    x = _moe(x, norm_g[0, 1].reshape(1, D_MODEL), sc2, sh2, g2, moe_router[0], moe_bias[0], moe_w_gu,
             moe_w_down, 0, sh_w_gu[0], sh_w_down[0], fg, final_norm=False)

    sh1, sc1, g1, sh2, sc2, g2 = mods(1)
    x = _pool_mixer(x, _rstd(x), norm_g[1, 0].reshape(1, D_MODEL), sc1, sh1, pool_w[0].astype(BF16),
                    pool_scale[0].reshape(1, D_MODEL), g1)
    y = _moe(x, norm_g[1, 1].reshape(1, D_MODEL), sc2, sh2, g2, moe_router[1], moe_bias[1], moe_w_gu,
             moe_w_down, 1, sh_w_gu[1], sh_w_down[1], fg, final_norm=True)

    y_prompt = y[:NP_TOK].reshape(BATCH, SEQ, D_MODEL)
    y_sample = y[NP_TOK:].reshape(DEC_BATCH, DEC_SEQ, D_MODEL)
    return (y_prompt, y_sample, new_state)
```

```python
import functools
import math

import jax
import jax.numpy as jnp
from jax import lax
from jax.experimental import pallas as pl
from jax.experimental.pallas import tpu as pltpu

F32 = jnp.float32
BF16 = jnp.bfloat16
I32 = jnp.int32
U32 = jnp.uint32

D_MODEL = 2048
BATCH = 16
SEQ = 256
DEC_BATCH = 8
DEC_SEQ = 2048
GRID_W = 64
NP_TOK = BATCH * SEQ
NS_TOK = DEC_BATCH * DEC_SEQ
N_TOK = NP_TOK + NS_TOK
N_MOD = 16

DN_DK = 128
DN_DV = 128
DN_NK = D_MODEL // DN_DK
DN_NV = 2 * DN_NK
DN_QK = DN_NK * DN_DK
DN_V = DN_NV * DN_DV
DN_CONV = 5
CHUNK = 64
DN_HKB = 4
N_GATE = 4 * DN_NV

POOL_WINDOWS = (2, 4, 8, 16)
POOL_GROUP = D_MODEL // len(POOL_WINDOWS)

N_EXPERTS = 64
TOP_K = 8
N_EXPERT_GROUPS = 8
GROUP_SIZE = N_EXPERTS // N_EXPERT_GROUPS
TOPK_GROUPS = 4
D_EXPERT = 512
D_SHARED = 512
ROUTED_SCALE = 2.5
EPS = 1e-6

MOE_BLK = 256
N_ASSIGN = N_TOK * TOP_K
N_BLOCKS = (N_ASSIGN + N_EXPERTS * (MOE_BLK - 1) + MOE_BLK - 1) // MOE_BLK
N_SLOTS = N_BLOCKS * MOE_BLK
HALF = D_MODEL // 2
COMB_TM = 128
VMEM_LIMIT = 56 * 1024 * 1024

_NT = (((1,), (1,)), ((), ()))
_TN = (((0,), (0,)), ((), ()))


def _nt(a, b, precision=None):
    return lax.dot_general(a, b, _NT, preferred_element_type=F32, precision=precision)


def _tn(a, b):
    return lax.dot_general(a, b, _TN, preferred_element_type=F32)


def _dot(a, b, precision=None):
    return jnp.dot(a, b, preferred_element_type=F32, precision=precision)


def _silu(x):
    return x * jax.nn.sigmoid(x)


def _mod_row(tile_start):
    return jnp.where(tile_start < NP_TOK, 0, 1 + (tile_start - NP_TOK) // DEC_SEQ)


def _params(sem, vmem=VMEM_LIMIT):
    return pltpu.CompilerParams(dimension_semantics=sem, vmem_limit_bytes=vmem)


def _pack_halves(y):
    a = lax.bitcast_convert_type(y[:, :HALF].astype(BF16).astype(F32), U32)
    b = lax.bitcast_convert_type(y[:, HALF:].astype(BF16).astype(F32), U32)
    return a | (b >> 16)


def _unpack_halves(u):
    a = lax.bitcast_convert_type(u & jnp.uint32(0xFFFF0000), F32)
    b = lax.bitcast_convert_type(u << 16, F32)
    return a, b


def _ada_kernel(c_ref, w_ref, b_ref, o_ref):
    a = _silu(c_ref[...]).astype(BF16)
    o_ref[...] = _dot(a, w_ref[...].astype(BF16)) + b_ref[...]


def _ada_mod(cvec, ada_w, ada_b):
    depth, d, n = ada_w.shape
    tn = 1024
    return pl.pallas_call(
        _ada_kernel,
        grid=(depth, n // tn),
        in_specs=[
            pl.BlockSpec((N_MOD, d), lambda l, j: (0, 0)),
            pl.BlockSpec((None, d, tn), lambda l, j: (l, 0, j)),
            pl.BlockSpec((None, 1, tn), lambda l, j: (l, 0, j)),
        ],
        out_specs=pl.BlockSpec((None, N_MOD, tn), lambda l, j: (l, 0, j)),
        out_shape=jax.ShapeDtypeStruct((depth, N_MOD, n), F32),
        compiler_params=_params(("parallel", "parallel")),
        name="ada_mod",
    )(cvec, ada_w, ada_b.reshape(depth, 1, n))


def _norm_mod(x, g, sc, sh):
    ms = jnp.mean(x * x, axis=-1, keepdims=True)
    return x * lax.rsqrt(ms + EPS) * g * (1.0 + sc) + sh


def _in_proj_kernel(x_ref, g_ref, sc_ref, sh_ref, w_ref, wab_ref, wab_lo_ref, o_ref, ab_ref, h_scr):
    @pl.when(pl.program_id(1) == 0)
    def _():
        hf = _norm_mod(x_ref[...], g_ref[...], sc_ref[...], sh_ref[...])
        h = hf.astype(BF16)
        h_scr[...] = h
        h_lo = (hf - h.astype(F32)).astype(BF16)
        ab_ref[...] = _dot(h, wab_ref[...]) + (_dot(h_lo, wab_ref[...]) + _dot(h, wab_lo_ref[...]))

    o_ref[...] = _dot(h_scr[...], w_ref[...]).astype(o_ref.dtype)


def _in_proj(x, g, sc, sh, w, wab_f32):
    wab = wab_f32.astype(BF16)
    wab_lo = (wab_f32 - wab.astype(F32)).astype(BF16)
    tm, tn = 1024, 1024
    n = w.shape[1]
    mod_spec = pl.BlockSpec((None, 1, D_MODEL), lambda i, j: (_mod_row(i * tm), 0, 0))
    return pl.pallas_call(
        _in_proj_kernel,
        grid=(N_TOK // tm, n // tn),
        in_specs=[
            pl.BlockSpec((tm, D_MODEL), lambda i, j: (i, 0)),
            pl.BlockSpec((1, D_MODEL), lambda i, j: (0, 0)),
            mod_spec,
            mod_spec,
            pl.BlockSpec((D_MODEL, tn), lambda i, j: (0, j)),
            pl.BlockSpec((D_MODEL, N_GATE), lambda i, j: (0, 0)),
            pl.BlockSpec((D_MODEL, N_GATE), lambda i, j: (0, 0)),
        ],
        out_specs=[
            pl.BlockSpec((tm, tn), lambda i, j: (i, j)),
            pl.BlockSpec((tm, N_GATE), lambda i, j: (i, 0)),
        ],
        out_shape=[
            jax.ShapeDtypeStruct((N_TOK, n), BF16),
            jax.ShapeDtypeStruct((N_TOK, N_GATE), F32),
        ],
        scratch_shapes=[pltpu.VMEM((tm, D_MODEL), BF16)],
        compiler_params=_params(("parallel", "arbitrary")),
        name="dn_in_proj",
    )(x, g, sc, sh, w, wab, wab_lo)


def _gates_kernel(ab_ref, alog_ref, dt_ref, o_ref):
    tm = ab_ref.shape[0]
    a = ab_ref[...]
    z = a + dt_ref[...]
    softplus = jnp.maximum(z, 0.0) + jnp.log(1.0 + jnp.exp(-jnp.abs(z)))
    g = -jnp.exp(alog_ref[...]) * softplus
    beta = jax.nn.sigmoid(a)
    ii = lax.broadcasted_iota(I32, (tm, tm), 0)
    jj = lax.broadcasted_iota(I32, (tm, tm), 1)
    same = (ii // CHUNK) == (jj // CHUNK)
    m_f = jnp.where(same & (jj <= ii), 1.0, 0.0).astype(F32)
    m_b = jnp.where(same & (jj >= ii), 1.0, 0.0).astype(F32)
    gc_f = _dot(m_f, g, precision=lax.Precision.HIGHEST)
    gc_b = _dot(m_b, g, precision=lax.Precision.HIGHEST)
    lane = lax.broadcasted_iota(I32, a.shape, 1)
    is_g = (lane % (2 * DN_NV)) < DN_NV
    o_ref[...] = jnp.where(is_g, jnp.where(lane < 2 * DN_NV, gc_f, gc_b), beta)


def _gates(ab, a_log, dt_bias):
    tm = 256
    zeros = jnp.zeros((DN_NV,), F32)
    alog = jnp.concatenate([a_log[0], zeros, a_log[1], zeros]).reshape(1, N_GATE)
    dt = jnp.concatenate([dt_bias[0], zeros, dt_bias[1], zeros]).reshape(1, N_GATE)
    return pl.pallas_call(
        _gates_kernel,
        grid=(N_TOK // tm,),
        in_specs=[
            pl.BlockSpec((tm, N_GATE), lambda i: (i, 0)),
            pl.BlockSpec((1, N_GATE), lambda i: (0, 0)),
            pl.BlockSpec((1, N_GATE), lambda i: (0, 0)),
        ],
        out_specs=pl.BlockSpec((tm, N_GATE), lambda i: (i, 0)),
        out_shape=jax.ShapeDtypeStruct((N_TOK, N_GATE), F32),
        compiler_params=_params(("parallel",)),
        name="dn_gates",
    )(ab, alog, dt)


CONV_ROWS = 2048
CONV_TC = 512


CONV_CHUNK = 128
CONV_EDGE = 16


def _conv_kernel(x_ref, w_ref, o_ref):
    i = pl.program_id(0)
    j = pl.program_id(1)
    rows = x_ref.shape[0]
    seq_len = jnp.where(i * rows < NP_TOK, SEQ, DEC_SEQ)
    half = DN_CONV // 2
    taps = [s for s in range(-half, half + 1) if s != 0]
    win_rows = CONV_CHUNK + 2 * CONV_EDGE
    q = lax.broadcasted_iota(I32, (len(taps) * CONV_CHUNK, win_rows), 0)
    col = lax.broadcasted_iota(I32, (len(taps) * CONV_CHUNK, win_rows), 1)
    src = CONV_EDGE + (q % CONV_CHUNK)
    for t, s in enumerate(taps):
        src = jnp.where(q // CONV_CHUNK == t, src + s, src)
    shift_mat = jnp.where(col == src, 1.0, 0.0).astype(BF16)
    is_qk = j < (2 * DN_QK) // CONV_TC
    qscale = jnp.where(j < DN_QK // CONV_TC, DN_DK ** -0.5, 1.0)

    def chunk(c, carry):
        r0 = pl.multiple_of(c * CONV_CHUNK, CONV_CHUNK)
        xc = x_ref[pl.ds(r0, CONV_CHUNK), :]
        r_prev = pl.multiple_of(jnp.maximum(r0 - CONV_EDGE, 0), CONV_EDGE)
        r_next = pl.multiple_of(jnp.minimum(r0 + CONV_CHUNK, rows - CONV_EDGE), CONV_EDGE)
        prev = x_ref[pl.ds(r_prev, CONV_EDGE), :]
        nxt = x_ref[pl.ds(r_next, CONV_EDGE), :]
        prev = jnp.where((r0 & (seq_len - 1)) != 0, prev, jnp.zeros_like(prev))
        nxt = jnp.where(((r0 + CONV_CHUNK) & (seq_len - 1)) != 0, nxt, jnp.zeros_like(nxt))
        shifted = _dot(shift_mat, jnp.concatenate([prev, xc, nxt], axis=0))
        acc = xc.astype(F32) * w_ref[half:half + 1, :]
        for t, s in enumerate(taps):
            acc = acc + shifted[t * CONV_CHUNK:(t + 1) * CONV_CHUNK] * w_ref[half + s:half + s + 1, :]
        y = _silu(acc)
        for h in range(CONV_TC // DN_DK):
            sl = slice(h * DN_DK, (h + 1) * DN_DK)
            yh = y[:, sl]
            inv = lax.rsqrt(jnp.sum(yh * yh, axis=-1, keepdims=True) + EPS) * qscale
            o_ref[pl.ds(r0, CONV_CHUNK), sl] = (yh * jnp.where(is_qk, inv, 1.0)).astype(o_ref.dtype)
        return carry

    lax.fori_loop(0, rows // CONV_CHUNK, chunk, 0, unroll=4)


def _conv_qkv(proj, conv_w):
    nch = 2 * DN_QK + DN_V
    return pl.pallas_call(
        _conv_kernel,
        grid=(N_TOK // CONV_ROWS, nch // CONV_TC),
        in_specs=[
            pl.BlockSpec((CONV_ROWS, CONV_TC), lambda i, j: (i, j)),
            pl.BlockSpec((DN_CONV, CONV_TC), lambda i, j: (0, j)),
        ],
        out_specs=pl.BlockSpec((CONV_ROWS, CONV_TC), lambda i, j: (i, j)),
        out_shape=jax.ShapeDtypeStruct((N_TOK, nch), BF16),
        compiler_params=_params(("parallel", "parallel")),
        name="dn_conv",
    )(proj, conv_w)


def _delta_kernel(*refs, n_chunks, has_s0, write_state):
    q_ref, k_ref, v_ref, gc_ref, rows_ref = refs[:5]
    pos = 5
    if has_s0:
        s0_ref = refs[pos]
        pos += 1
    o_ref = refs[pos]
    pos += 1
    if write_state:
        sfin_ref = refs[pos]
        pos += 1
    s_scr = refs[pos]
    out_ref = o_ref
    o_ref = refs[pos + 1]

    hk0 = pl.program_id(1) * DN_HKB
    if has_s0:
        for hl in range(DN_HKB):
            for d in range(2):
                for r in range(2):
                    s_scr[4 * hl + 2 * d + r] = s0_ref[d, 2 * hl + r]
    else:
        s_scr[...] = jnp.zeros(s_scr.shape, F32)
    o_ref[...] = jnp.zeros(o_ref.shape, F32)

    ii = lax.broadcasted_iota(I32, (CHUNK, CHUNK), 0)
    jj = lax.broadcasted_iota(I32, (CHUNK, CHUNK), 1)
    eye = jnp.where(ii == jj, 1.0, 0.0).astype(F32)
    lane = lax.broadcasted_iota(I32, (CHUNK, N_GATE), 1)
    pair_masks = [((ii >> (l + 1)) == (jj >> (l + 1))) & ((ii >> l) != (jj >> l))
                  for l in range(int(math.log2(CHUNK)))]

    chain_ids = [(hl, d, r) for hl in range(DN_HKB) for d in range(2) for r in range(2)]

    def chunk_index(t, d):
        return t if d == 0 else n_chunks - 1 - t

    def local_begin(t):
        chains = []
        for hl in range(DN_HKB):
            for d in range(2):
                c = chunk_index(t, d)
                r0 = pl.multiple_of(c * CHUNK, CHUNK)
                k = k_ref[pl.ds(r0, CHUNK), hl * DN_DK:(hl + 1) * DN_DK]
                q = q_ref[pl.ds(r0, CHUNK), hl * DN_DK:(hl + 1) * DN_DK]
                gates = gc_ref[pl.ds(r0, CHUNK), :]
                rows = rows_ref[hl, c]
                kk = _nt(k, k)
                qk = _nt(q, k)
                incl = (jj <= ii) if d == 0 else (jj >= ii)
                strict = (jj < ii) if d == 0 else (jj > ii)
                last = CHUNK - 1 if d == 0 else 0
                for r in range(2):
                    lane_g = d * 2 * DN_NV + 2 * (hk0 + hl) + r
                    gcc = jnp.sum(jnp.where(lane == lane_g, gates, 0.0), axis=1, keepdims=True)
                    bc = jnp.sum(jnp.where(lane == lane_g + DN_NV, gates, 0.0), axis=1, keepdims=True)
                    gcr = rows[2 * d + r:2 * d + r + 1, :]
                    gl = gcr[:, last:last + 1]
                    dec = jnp.where(incl, jnp.exp(jnp.where(incl, gcc - gcr, 0.0)), 0.0)
                    lm = jnp.where(strict, bc * kk * dec, 0.0)
                    v = v_ref[pl.ds(r0, CHUNK), (2 * hl + r) * DN_DV:(2 * hl + r + 1) * DN_DV]
                    chains.append(dict(k=k, q=q, v=v, gcc=gcc, bc=bc, gl=gl, attn=(qk * dec).astype(BF16),
                                       lb=lm.astype(BF16), tm=eye - jnp.where(pair_masks[0], lm, 0.0)))
        return chains

    def local_level(chains, lvl):
        for ch in chains:
            ch["tb"] = ch["tm"].astype(BF16)
            ch["bt"] = _dot(jnp.where(pair_masks[lvl], ch["lb"], jnp.zeros_like(ch["lb"])), ch["tb"])
        for ch in chains:
            ch["tm"] = ch["tm"] - _dot(ch["tb"], ch["bt"].astype(BF16))

    def local_finish(chains):
        for ci, ch in enumerate(chains):
            kf = ch["k"].astype(F32)
            egc = jnp.exp(ch["gcc"])
            rhs = jnp.concatenate([(ch["v"].astype(F32) * ch["bc"]).astype(BF16),
                                   (kf * (ch["bc"] * egc)).astype(BF16)], axis=1)
            ch["uw"] = _dot(ch["tm"].astype(BF16), rhs)
            ch["qg"] = (ch["q"].astype(F32) * egc).astype(BF16)
            ch["kd"] = (kf * jnp.exp(ch["gl"] - ch["gcc"])).astype(BF16)

    def scan_state_products(chains):
        for ci, ch in enumerate(chains):
            ch["s"] = s_scr[ci]
            wq = jnp.concatenate([ch["uw"][:, DN_DV:].astype(BF16), ch["qg"]], axis=0)
            ch["ws_qs"] = _dot(wq, ch["s"].astype(BF16))

    def scan_update(t, chains):
        for ci, (hl, d, r) in enumerate(chain_ids):
            ch = chains[ci]
            r0 = pl.multiple_of(chunk_index(t, d) * CHUNK, CHUNK)
            v_new = (ch["uw"][:, :DN_DV] - ch["ws_qs"][:CHUNK]).astype(BF16)
            o = ch["ws_qs"][CHUNK:] + _dot(ch["attn"], v_new)
            s_scr[ci] = ch["s"] * jnp.exp(ch["gl"]) + _tn(ch["kd"], v_new)
            oc = 2 * hl + r
            o_ref[pl.ds(r0, CHUNK), oc * DN_DV:(oc + 1) * DN_DV] += o

    def body(t, carry):
        chains = local_begin(t)
        for lvl in range(1, len(pair_masks)):
            local_level(chains, lvl)
        local_finish(chains)
        scan_state_products(chains)
        scan_update(t, chains)
        return carry

    lax.fori_loop(0, n_chunks, body, 0)

    def emit(c, carry):
        r0 = pl.multiple_of(c * CHUNK, CHUNK)
        out_ref[pl.ds(r0, CHUNK), :] = o_ref[pl.ds(r0, CHUNK), :].astype(out_ref.dtype)
        return carry

    lax.fori_loop(0, n_chunks, emit, 0)
    if write_state:
        for hl in range(DN_HKB):
            for d in range(2):
                for r in range(2):
                    sfin_ref[d, 2 * hl + r] = s_scr[4 * hl + 2 * d + r]


def _delta(qkv, gc, rows, s0, *, n_seq, seq_len, row_blk0, write_state):
    n_chunks = seq_len // CHUNK
    has_s0 = s0 is not None
    kern = functools.partial(_delta_kernel, n_chunks=n_chunks, has_s0=has_s0, write_state=write_state)
    ngrp = DN_NK // DN_HKB
    qk_w = DN_HKB * DN_DK
    v_w = 2 * DN_HKB * DN_DV
    in_specs = [
        pl.BlockSpec((seq_len, qk_w), lambda b, h: (row_blk0 + b, h)),
        pl.BlockSpec((seq_len, qk_w), lambda b, h: (row_blk0 + b, ngrp + h)),
        pl.BlockSpec((seq_len, v_w), lambda b, h: (row_blk0 + b, ngrp + h)),
        pl.BlockSpec((seq_len, N_GATE), lambda b, h: (row_blk0 + b, 0)),
        pl.BlockSpec((None, DN_HKB, n_chunks, 8, CHUNK), lambda b, h: (b, h, 0, 0, 0)),
    ]
    args = [qkv, qkv, qkv, gc, rows]
    state_spec = pl.BlockSpec((None, None, 2, 2 * DN_HKB, DN_DK, DN_DV), lambda b, h: (b, 0, 0, h, 0, 0))
    if has_s0:
        in_specs.append(state_spec)
        args.append(s0)
    out_specs = [pl.BlockSpec((seq_len, v_w), lambda b, h: (b, h))]
    out_shape = [jax.ShapeDtypeStruct((n_seq * seq_len, DN_V), BF16)]
    if write_state:
        out_specs.append(state_spec)
        out_shape.append(jax.ShapeDtypeStruct((n_seq, 1, 2, DN_NV, DN_DK, DN_DV), F32))
    return pl.pallas_call(
        kern,
        grid=(n_seq, ngrp),
        in_specs=in_specs,
        out_specs=out_specs,
        out_shape=out_shape,
        scratch_shapes=[pltpu.VMEM((4 * DN_HKB, DN_DK, DN_DV), F32), pltpu.VMEM((seq_len, v_w), F32)],
        compiler_params=_params(("parallel", "parallel")),
        name="dn_delta_%d" % seq_len,
    )(*args)


def _gate_rows(gc, n_seq, seq_len):
    n_chunks = seq_len // CHUNK
    g = gc.reshape(n_seq, n_chunks, CHUNK, 2, 2, DN_NK, 2)
    g = g.transpose(0, 5, 1, 4, 3, 6, 2)
    return g.reshape(n_seq, DN_NK, n_chunks, 8, CHUNK)


def _deltanet_core(x, g, sc1, sh1, w_in, conv_w, a_log, dt_bias, state_dn, sample=True):
    nqkv = 2 * DN_QK + 2 * DN_V
    proj, ab = _in_proj(x, g, sc1, sh1, w_in[:, :nqkv].astype(BF16), w_in[:, nqkv:])
    gc = _gates(ab, a_log, dt_bias)
    qkv = _conv_qkv(proj, conv_w)
    o_p, new_state = _delta(qkv, gc, _gate_rows(gc[:NP_TOK], BATCH, SEQ), None,
                            n_seq=BATCH, seq_len=SEQ, row_blk0=0, write_state=True)
    if not sample:
        return o_p, proj, new_state
    (o_s,) = _delta(qkv, gc, _gate_rows(gc[NP_TOK:], DEC_BATCH, DEC_SEQ), state_dn,
                    n_seq=DEC_BATCH, seq_len=DEC_SEQ, row_blk0=NP_TOK // DEC_SEQ, write_state=False)
    return jnp.concatenate([o_p, o_s], axis=0), proj, new_state


def _dn_out_kernel(o_ref, z_ref, og_ref, w_ref, x_ref, g1_ref, out_ref, a_scr):
    @pl.when(pl.program_id(1) == 0)
    def _():
        for h in range(DN_NV):
            sl = slice(h * DN_DV, (h + 1) * DN_DV)
            o = o_ref[:, sl].astype(F32)
            z = z_ref[:, sl].astype(F32)
            on = o * lax.rsqrt(jnp.mean(o * o, axis=-1, keepdims=True) + EPS) * og_ref[...]
            a_scr[:, sl] = (on * _silu(z)).astype(BF16)

    out_ref[...] = x_ref[...] + g1_ref[...] * _dot(a_scr[...], w_ref[...])


def _dn_out(o, proj, onorm_g, w_out, x, g1):
    tm, tn = 256, D_MODEL
    zblk = (2 * DN_QK + DN_V) // DN_V
    return pl.pallas_call(
        _dn_out_kernel,
        grid=(N_TOK // tm, D_MODEL // tn),
        in_specs=[
            pl.BlockSpec((tm, DN_V), lambda i, j: (i, 0)),
            pl.BlockSpec((tm, DN_V), lambda i, j: (i, zblk)),
            pl.BlockSpec((1, DN_DV), lambda i, j: (0, 0)),
            pl.BlockSpec((DN_V, tn), lambda i, j: (0, j), pipeline_mode=pl.Buffered(1)),
            pl.BlockSpec((tm, tn), lambda i, j: (i, j)),
            pl.BlockSpec((None, 1, tn), lambda i, j: (_mod_row(i * tm), 0, j)),
        ],
        out_specs=pl.BlockSpec((tm, tn), lambda i, j: (i, j)),
        out_shape=jax.ShapeDtypeStruct((N_TOK, D_MODEL), F32),
        scratch_shapes=[pltpu.VMEM((tm, DN_V), BF16)],
        compiler_params=_params(("parallel", "arbitrary")),
        name="dn_out",
    )(o, proj, onorm_g, w_out, x, g1)


def _rstd_kernel(x_ref, o_ref):
    x = x_ref[...]
    o_ref[...] = lax.rsqrt(jnp.mean(x * x, axis=-1, keepdims=True) + EPS)


def _rstd(x):
    tm = 1024
    return pl.pallas_call(
        _rstd_kernel,
        grid=(N_TOK // tm,),
        in_specs=[pl.BlockSpec((tm, D_MODEL), lambda i: (i, 0))],
        out_specs=pl.BlockSpec((tm, 1), lambda i: (i, 0)),
        out_shape=jax.ShapeDtypeStruct((N_TOK, 1), F32),
        compiler_params=_params(("parallel",)),
        name="rstd",
    )(x)


POOL_ROWS = 2048


def _window_sum(x, pos, length, stride, w):
    rows = x.shape[0]
    m = w // 2

    def shifted(a, off):
        valid = (pos + off >= 0) & (pos + off < length)
        return jnp.where(valid, pltpu.roll(a, (-off * stride) % rows, 0), 0.0)

    fwd = x
    bwd = x
    step = 1
    while step < m:
        fwd = fwd + shifted(fwd, step)
        bwd = bwd + shifted(bwd, -step)
        step *= 2
    return fwd + shifted(bwd, -1)


def _window_count(pos, length, w):
    lo = jnp.maximum(pos - w // 2, 0)
    hi = jnp.minimum(pos + (w - w // 2) - 1, length - 1)
    return (hi - lo + 1).astype(F32)


def _pool_kernel(x_ref, rstd_ref, g_ref, sc_ref, sh_ref, w_ref, ps_ref, g1_ref, o_ref):
    i = pl.program_id(0)
    grp = pl.program_id(1)
    x = x_ref[...]
    h = x * rstd_ref[...] * g_ref[...] * (1.0 + sc_ref[...]) + sh_ref[...]
    ridx = lax.broadcasted_iota(I32, (POOL_ROWS, 1), 0)

    def finish(pooled):
        p = (pooled - h).astype(BF16)
        y = _dot(p, w_ref[...]) * ps_ref[...]
        o_ref[...] = x + g1_ref[...] * y

    for gi, w in enumerate(POOL_WINDOWS):
        @pl.when((grp == gi) & (i * POOL_ROWS < NP_TOK))
        def _(w=w):
            pos = ridx & (SEQ - 1)
            s = _window_sum(h, pos, SEQ, 1, w)
            finish(s / _window_count(pos, SEQ, w))

        @pl.when((grp == gi) & (i * POOL_ROWS >= NP_TOK))
        def _(w=w):
            col = ridx & (GRID_W - 1)
            row = ridx // GRID_W
            n_rows = POOL_ROWS // GRID_W
            s = _window_sum(h, col, GRID_W, 1, w)
            s = _window_sum(s, row, n_rows, GRID_W, w)
            finish(s / (_window_count(col, GRID_W, w) * _window_count(row, n_rows, w)))


def _pool_mixer(x, rstd, g, sc, sh, pool_w, pool_scale, g1):
    ngrp = len(POOL_WINDOWS)
    mod = lambda i, j: (_mod_row(i * POOL_ROWS), 0, j)
    return pl.pallas_call(
        _pool_kernel,
        grid=(N_TOK // POOL_ROWS, ngrp),
        in_specs=[
            pl.BlockSpec((POOL_ROWS, POOL_GROUP), lambda i, j: (i, j)),
            pl.BlockSpec((POOL_ROWS, 1), lambda i, j: (i, 0)),
            pl.BlockSpec((1, POOL_GROUP), lambda i, j: (0, j)),
            pl.BlockSpec((None, 1, POOL_GROUP), mod),
            pl.BlockSpec((None, 1, POOL_GROUP), mod),
            pl.BlockSpec((None, POOL_GROUP, POOL_GROUP), lambda i, j: (j, 0, 0)),
            pl.BlockSpec((1, POOL_GROUP), lambda i, j: (0, j)),
            pl.BlockSpec((None, 1, POOL_GROUP), mod),
        ],
        out_specs=pl.BlockSpec((POOL_ROWS, POOL_GROUP), lambda i, j: (i, j)),
        out_shape=jax.ShapeDtypeStruct((N_TOK, D_MODEL), F32),
        compiler_params=_params(("parallel", "parallel")),
        name="pool_mixer",
    )(x, rstd, g, sc, sh, pool_w, pool_scale, g1)


def _moe_pre_kernel(x_ref, g_ref, sc_ref, sh_ref, rw_ref, rb_ref, hp_ref, idx_ref, wt_ref):
    tm = x_ref.shape[0]
    h = _norm_mod(x_ref[...], g_ref[...], sc_ref[...], sh_ref[...])
    hp_ref[...] = _pack_halves(h)
    logits = _nt(rw_ref[...], h, precision=lax.Precision.HIGHEST)
    s = jax.nn.sigmoid(logits)
    choice = s + rb_ref[...]
    neg = -jnp.inf
    row8 = lax.broadcasted_iota(I32, (GROUP_SIZE, tm), 0)
    rowg = lax.broadcasted_iota(I32, (N_EXPERT_GROUPS, tm), 0)

    def first_max(vals, riota, n):
        m = jnp.max(vals, axis=0, keepdims=True)
        i = jnp.min(jnp.where(vals == m, riota, n), axis=0, keepdims=True)
        return m, i

    gscore = jnp.zeros((N_EXPERT_GROUPS, tm), F32)
    for g in range(N_EXPERT_GROUPS):
        cg = choice[g * GROUP_SIZE:(g + 1) * GROUP_SIZE, :]
        m1, i1 = first_max(cg, row8, GROUP_SIZE)
        m2 = jnp.max(jnp.where(row8 == i1, neg, cg), axis=0, keepdims=True)
        gscore = jnp.where(rowg == g, m1 + m2, gscore)
    gsel = jnp.zeros((N_EXPERT_GROUPS, tm), F32)
    for _ in range(TOPK_GROUPS):
        _, i = first_max(gscore, rowg, N_EXPERT_GROUPS)
        gsel = jnp.where(rowg == i, 1.0, gsel)
        gscore = jnp.where(rowg == i, neg, gscore)
    masked = jnp.concatenate(
        [jnp.where(gsel[g:g + 1, :] > 0.0, choice[g * GROUP_SIZE:(g + 1) * GROUP_SIZE, :], neg)
         for g in range(N_EXPERT_GROUPS)], axis=0)
    rowe = lax.broadcasted_iota(I32, (N_EXPERTS, tm), 0)
    rowk = lax.broadcasted_iota(I32, (TOP_K, tm), 0)
    idx = jnp.zeros((TOP_K, tm), I32)
    wts = jnp.zeros((TOP_K, tm), F32)
    for kk in range(TOP_K):
        _, i = first_max(masked, rowe, N_EXPERTS)
        sel = rowe == i
        wk = jnp.sum(jnp.where(sel, s, 0.0), axis=0, keepdims=True)
        idx = jnp.where(rowk == kk, i, idx)
        wts = jnp.where(rowk == kk, wk, wts)
        masked = jnp.where(sel, neg, masked)
    wts = wts / jnp.sum(wts, axis=0, keepdims=True) * ROUTED_SCALE
    idx_ref[...] = idx
    wt_ref[...] = wts


def _moe_pre(x, g, sc, sh, router_wt, router_b):
    tm = 256
    mod_spec = pl.BlockSpec((None, 1, D_MODEL), lambda i: (_mod_row(i * tm), 0, 0))
    return pl.pallas_call(
        _moe_pre_kernel,
        grid=(N_TOK // tm,),
        in_specs=[
            pl.BlockSpec((tm, D_MODEL), lambda i: (i, 0)),
            pl.BlockSpec((1, D_MODEL), lambda i: (0, 0)),
            mod_spec,
            mod_spec,
            pl.BlockSpec((N_EXPERTS, D_MODEL), lambda i: (0, 0)),
            pl.BlockSpec((N_EXPERTS, 1), lambda i: (0, 0)),
        ],
        out_specs=[
            pl.BlockSpec((tm, HALF), lambda i: (i, 0)),
            pl.BlockSpec((TOP_K, tm), lambda i: (0, i)),
            pl.BlockSpec((TOP_K, tm), lambda i: (0, i)),
        ],
        out_shape=[
            jax.ShapeDtypeStruct((N_TOK, HALF), U32),
            jax.ShapeDtypeStruct((TOP_K, N_TOK), I32),
            jax.ShapeDtypeStruct((TOP_K, N_TOK), F32),
        ],
        compiler_params=_params(("parallel",)),
        name="moe_pre",
    )(x, g, sc, sh, router_wt, router_b)


def _route(idx_t):
    flat_e = idx_t.T.reshape(-1)
    iota = jnp.arange(N_ASSIGN, dtype=I32)
    e_sorted, order = lax.sort_key_val(flat_e, iota)
    _, inv_order = lax.sort_key_val(order, iota)
    start = jnp.searchsorted(e_sorted, jnp.arange(N_EXPERTS, dtype=I32), side="left", method="scan").astype(I32)
    counts = jnp.concatenate([start[1:], jnp.full((1,), N_ASSIGN, I32)]) - start
    padded = (counts + MOE_BLK - 1) // MOE_BLK * MOE_BLK
    pend = jnp.cumsum(padded)
    pstart = pend - padded
    shift = pstart - start
    dshift = shift - jnp.concatenate([jnp.zeros((1,), I32), shift[:-1]])
    steps = jnp.zeros((N_ASSIGN,), I32).at[start].add(dshift, mode="drop")
    dest_sorted = iota + jnp.cumsum(steps)
    pos = dest_sorted[inv_order]
    block_start = jnp.arange(N_BLOCKS, dtype=I32) * MOE_BLK
    block_e = jnp.minimum(jnp.sum(pend[None, :] <= block_start[:, None], axis=1), N_EXPERTS - 1).astype(I32)
    slot = block_start[:, None] + jnp.arange(MOE_BLK, dtype=I32)[None, :]
    within = slot - pstart[block_e][:, None]
    valid = (within < counts[block_e][:, None]) & (slot < pend[N_EXPERTS - 1])
    src = jnp.clip(slot - shift[block_e][:, None], 0, N_ASSIGN - 1)
    slot_tok = jnp.where(valid, order[src] // TOP_K, 0).astype(I32)
    n_used = (pend[N_EXPERTS - 1] // MOE_BLK).astype(I32).reshape(1)
    return slot_tok.reshape(N_BLOCKS, MOE_BLK), pos.astype(I32), block_e, n_used


GATHER_DEPTH = 3
GATHER_IDX_ROWS = 4


def _gather_pipeline(step, n_steps, idx_hbm, src_hbm, idx_smem, bufs, sem_idx, sem_rows, n_rows):
    last = idx_hbm.shape[0] - 1
    nbuf = GATHER_DEPTH

    def idx_copy(blk, slot):
        return pltpu.make_async_copy(idx_hbm.at[jnp.minimum(blk, last)], idx_smem.at[slot], sem_idx.at[slot])

    def row_copy(slot, r, row):
        return pltpu.make_async_copy(src_hbm.at[pl.ds(row, 1)], bufs[slot].at[pl.ds(r, 1)], sem_rows.at[slot])

    def wait_rows(slot):
        pltpu.make_async_copy(src_hbm.at[pl.ds(0, n_rows)], bufs[slot], sem_rows.at[slot]).wait()

    @pl.when(step == 0)
    def _():
        for s in range(nbuf - 1):
            idx_copy(s, s).start()
            idx_copy(s, s).wait()

            def one(r, carry, s=s):
                row_copy(s, r, idx_smem[s, r]).start(priority=s % 2)
                return carry
            lax.fori_loop(0, n_rows, one, 0, unroll=8)
        idx_copy(nbuf - 1, nbuf - 1).start()

    def run(compute):
        for cur in range(nbuf):
            new = (cur + nbuf - 1) % nbuf

            @pl.when((step < n_steps) & (step % nbuf == cur))
            def _(cur=cur, new=new):
                idx_copy(step + nbuf - 1, new).wait()
                wait_rows(cur)
                for r in range(n_rows):
                    row_copy(new, r, idx_smem[new, r]).start(priority=r % 2)
                idx_copy(step + nbuf, cur).start()
                compute(bufs[cur])

                @pl.when(step == n_steps - 1)
                def _():
                    for s in range(1, nbuf):
                        wait_rows((cur + s) % nbuf)
                    idx_copy(last, cur).wait()

    return run


def _expert_kernel(be_ref, nu_ref, tok_hbm, hp_hbm, wgu_ref, wd_ref, y_ref, idx_smem, xbuf0, xbuf1, xbuf2,
                   wgu_bf, wd_bf, sem_idx, sem_rows):
    b = pl.program_id(0)
    nu = nu_ref[0]
    run = _gather_pipeline(b, nu, tok_hbm, hp_hbm, idx_smem, (xbuf0, xbuf1, xbuf2), sem_idx, sem_rows, MOE_BLK)

    @pl.when((b == 0) | (be_ref[b] != be_ref[jnp.maximum(b - 1, 0)]))
    def _():
        rows = 256

        def cast_gu(i, carry):
            r = pl.multiple_of(i * rows, rows)
            wgu_bf[pl.ds(r, rows), :] = wgu_ref[pl.ds(r, rows), :].astype(BF16)
            return carry
        lax.fori_loop(0, D_MODEL // rows, cast_gu, 0)

        def cast_d(i, carry):
            r = pl.multiple_of(i * rows, rows)
            wd_bf[pl.ds(r, rows), :] = wd_ref[pl.ds(r, rows), :].astype(BF16)
            return carry
        lax.fori_loop(0, D_EXPERT // rows, cast_d, 0)

    def compute(xbuf):
        xa, xb = _unpack_halves(xbuf[...])
        gu = _dot(xa.astype(BF16), wgu_bf[:HALF, :]) + _dot(xb.astype(BF16), wgu_bf[HALF:, :])
        act = (_silu(gu[:, :D_EXPERT]) * gu[:, D_EXPERT:]).astype(BF16)
        y_ref[...] = _pack_halves(_dot(act, wd_bf[...]))

    run(compute)

    @pl.when(b >= nu)
    def _():
        y_ref[...] = jnp.zeros(y_ref.shape, U32)


def _experts(slot_tok, block_e, n_used, hp, w_gu, w_down, layer):
    grid_spec = pltpu.PrefetchScalarGridSpec(
        num_scalar_prefetch=2,
        grid=(N_BLOCKS,),
        in_specs=[
            pl.BlockSpec(memory_space=pl.ANY),
            pl.BlockSpec(memory_space=pl.ANY),
            pl.BlockSpec((None, None, D_MODEL, 2 * D_EXPERT), lambda b, be, nu: (layer, be[b], 0, 0)),
            pl.BlockSpec((None, None, D_EXPERT, D_MODEL), lambda b, be, nu: (layer, be[b], 0, 0)),
        ],
        out_specs=pl.BlockSpec((MOE_BLK, HALF), lambda b, be, nu: (b, 0)),
        scratch_shapes=[
            pltpu.SMEM((GATHER_IDX_ROWS, MOE_BLK), I32),
            pltpu.VMEM((MOE_BLK, HALF), U32),
            pltpu.VMEM((MOE_BLK, HALF), U32),
            pltpu.VMEM((MOE_BLK, HALF), U32),
            pltpu.VMEM((D_MODEL, 2 * D_EXPERT), BF16),
            pltpu.VMEM((D_EXPERT, D_MODEL), BF16),
            pltpu.SemaphoreType.DMA((GATHER_DEPTH,)),
            pltpu.SemaphoreType.DMA((GATHER_DEPTH,)),
        ],
    )
    return pl.pallas_call(
        _expert_kernel,
        grid_spec=grid_spec,
        out_shape=jax.ShapeDtypeStruct((N_SLOTS, HALF), U32),
        compiler_params=_params(("arbitrary",)),
        name="moe_experts",
    )(block_e, n_used, slot_tok, hp, w_gu, w_down)


COMB_ROWS = COMB_TM * TOP_K


def _combine_kernel(pos_hbm, y_hbm, x_ref, hp_ref, wt_ref, g2_ref, sgu_ref, sd_ref, fg_ref, o_ref,
                    idx_smem, ybuf0, ybuf1, ybuf2, sem_idx, sem_rows, *, final_norm):
    run = _gather_pipeline(pl.program_id(0), pl.num_programs(0), pos_hbm, y_hbm, idx_smem, (ybuf0, ybuf1, ybuf2),
                           sem_idx, sem_rows, COMB_ROWS)

    def compute(ybuf):
        ha, hb = _unpack_halves(hp_ref[...])
        gu = _dot(ha.astype(BF16), sgu_ref[:HALF, :]) + _dot(hb.astype(BF16), sgu_ref[HALF:, :])
        act = (_silu(gu[:, :D_SHARED]) * gu[:, D_SHARED:]).astype(BF16)
        ffn = _dot(act, sd_ref[...])
        acc_a = ffn[:, :HALF]
        acc_b = ffn[:, HALF:]
        for k in range(TOP_K):
            ya, yb = _unpack_halves(ybuf[k * COMB_TM:(k + 1) * COMB_TM, :])
            wk = wt_ref[:, k:k + 1]
            acc_a = acc_a + wk * ya
            acc_b = acc_b + wk * yb
        xa = x_ref[:, :HALF] + g2_ref[:, :HALF] * acc_a
        xb = x_ref[:, HALF:] + g2_ref[:, HALF:] * acc_b
        if final_norm:
            ms = (jnp.sum(xa * xa, axis=-1, keepdims=True) + jnp.sum(xb * xb, axis=-1, keepdims=True)) / D_MODEL
            inv = lax.rsqrt(ms + EPS)
            xa = xa * inv * fg_ref[:, :HALF]
            xb = xb * inv * fg_ref[:, HALF:]
        o_ref[:, :HALF] = xa
        o_ref[:, HALF:] = xb

    run(compute)


def _combine(pos, y, x, hp, wts, g2, sh_gu, sh_down, final_g, *, final_norm, tok0=0, n_tok=N_TOK):
    n_tiles = n_tok // COMB_TM
    t0 = tok0 // COMB_TM
    pos_t = pos.reshape(N_TOK // COMB_TM, COMB_TM, TOP_K)[t0:t0 + n_tiles]
    pos_t = pos_t.transpose(0, 2, 1).reshape(n_tiles, COMB_ROWS)
    kern = functools.partial(_combine_kernel, final_norm=final_norm)
    return pl.pallas_call(
        kern,
        grid=(n_tiles,),
        in_specs=[
            pl.BlockSpec(memory_space=pl.ANY),
            pl.BlockSpec(memory_space=pl.ANY),
            pl.BlockSpec((COMB_TM, D_MODEL), lambda i: (t0 + i, 0)),
            pl.BlockSpec((COMB_TM, HALF), lambda i: (t0 + i, 0)),
            pl.BlockSpec((COMB_TM, TOP_K), lambda i: (t0 + i, 0)),
            pl.BlockSpec((None, 1, D_MODEL), lambda i: (_mod_row((t0 + i) * COMB_TM), 0, 0)),
            pl.BlockSpec((D_MODEL, 2 * D_SHARED), lambda i: (0, 0)),
            pl.BlockSpec((D_SHARED, D_MODEL), lambda i: (0, 0)),
            pl.BlockSpec((1, D_MODEL), lambda i: (0, 0)),
        ],
        out_specs=pl.BlockSpec((COMB_TM, D_MODEL), lambda i: (i, 0)),
        out_shape=jax.ShapeDtypeStruct((n_tok, D_MODEL), F32),
        scratch_shapes=[
            pltpu.SMEM((GATHER_IDX_ROWS, COMB_ROWS), I32),
            pltpu.VMEM((COMB_ROWS, HALF), U32),
            pltpu.VMEM((COMB_ROWS, HALF), U32),
            pltpu.VMEM((COMB_ROWS, HALF), U32),
            pltpu.SemaphoreType.DMA((GATHER_DEPTH,)),
            pltpu.SemaphoreType.DMA((GATHER_DEPTH,)),
        ],
        compiler_params=_params(("arbitrary",)),
        name="moe_combine",
    )(pos_t, y, x, hp, wts, g2, sh_gu, sh_down, final_g)


def _moe(x, g, sc, sh, g2, router_w, router_b, w_gu, w_down, layer, sh_gu, sh_down, final_g, *, final_norm):
    hp, idx_t, wt_t = _moe_pre(x, g, sc, sh, router_w.T, router_b.reshape(N_EXPERTS, 1))
    slot_tok, pos, block_e, n_used = _route(idx_t)
    y = _experts(slot_tok, block_e, n_used, hp, w_gu, w_down, layer)
    args = (pos, y, x, hp, wt_t.T, g2, sh_gu.astype(BF16), sh_down.astype(BF16), final_g)
    if not final_norm:
        return _combine(*args, final_norm=False)
    return (_combine(*args, final_norm=True, tok0=0, n_tok=NP_TOK),
            _combine(*args, final_norm=True, tok0=NP_TOK, n_tok=NS_TOK))


def kernel(x_prompt, x_sample, state_dn, c, c_ctx, ada_w, ada_b, norm_g, final_g, dn_w_in, dn_conv, dn_a_log,
           dn_dt_bias, dn_onorm_g, dn_w_out, pool_w, pool_scale, moe_router, moe_bias, moe_w_gu, moe_w_down,
           sh_w_gu, sh_w_down):
    x = jnp.concatenate([x_prompt.reshape(NP_TOK, D_MODEL), x_sample.reshape(NS_TOK, D_MODEL)], axis=0)
    cvec = jnp.concatenate([c_ctx[None, :], c, jnp.zeros((N_MOD - 1 - DEC_BATCH, D_MODEL), F32)], axis=0)
    mod = _ada_mod(cvec, ada_w, ada_b)
    mod = mod.reshape(mod.shape[0], N_MOD, 6, 1, D_MODEL)
    fg = final_g.reshape(1, D_MODEL)

    def mods(layer):
        return [mod[layer, :, p] for p in range(6)]

    sh1, sc1, g1, sh2, sc2, g2 = mods(0)
    o, proj, new_state = _deltanet_core(x, norm_g[0, 0].reshape(1, D_MODEL), sc1, sh1, dn_w_in[0], dn_conv[0],
                                        dn_a_log[0], dn_dt_bias[0], state_dn)
    x = _dn_out(o, proj, dn_onorm_g[0].reshape(1, DN_DV), dn_w_out[0].astype(BF16), x, g1)
    x = _moe(x, norm_g[0, 1].reshape(1, D_MODEL), sc2, sh2, g2, moe_router[0], moe_bias[0], moe_w_gu,
             moe_w_down, 0, sh_w_gu[0], sh_w_down[0], fg, final_norm=False)

    sh1, sc1, g1, sh2, sc2, g2 = mods(1)
    x = _pool_mixer(x, _rstd(x), norm_g[1, 0].reshape(1, D_MODEL), sc1, sh1, pool_w[0].astype(BF16),
                    pool_scale[0].reshape(1, D_MODEL), g1)
    yp, ys = _moe(x, norm_g[1, 1].reshape(1, D_MODEL), sc2, sh2, g2, moe_router[1], moe_bias[1], moe_w_gu,
                  moe_w_down, 1, sh_w_gu[1], sh_w_down[1], fg, final_norm=True)
    return (yp.reshape(BATCH, SEQ, D_MODEL), ys.reshape(DEC_BATCH, DEC_SEQ, D_MODEL), new_state)
```

```python
import functools
import math

import jax
import jax.numpy as jnp
from jax import lax
from jax.experimental import pallas as pl
from jax.experimental.pallas import tpu as pltpu

F32 = jnp.float32
BF16 = jnp.bfloat16
I32 = jnp.int32
U32 = jnp.uint32

D_MODEL = 2048
BATCH = 16
SEQ = 256
DEC_BATCH = 8
DEC_SEQ = 2048
GRID_W = 64
NP_TOK = BATCH * SEQ
NS_TOK = DEC_BATCH * DEC_SEQ
N_TOK = NP_TOK + NS_TOK
N_MOD = 16

DN_DK = 128
DN_DV = 128
DN_NK = D_MODEL // DN_DK
DN_NV = 2 * DN_NK
DN_QK = DN_NK * DN_DK
DN_V = DN_NV * DN_DV
DN_CONV = 5
CHUNK = 64
DN_HKB = 4
N_GATE = 4 * DN_NV

POOL_WINDOWS = (2, 4, 8, 16)
POOL_GROUP = D_MODEL // len(POOL_WINDOWS)

N_EXPERTS = 64
TOP_K = 8
N_EXPERT_GROUPS = 8
GROUP_SIZE = N_EXPERTS // N_EXPERT_GROUPS
TOPK_GROUPS = 4
D_EXPERT = 512
D_SHARED = 512
ROUTED_SCALE = 2.5
EPS = 1e-6

MOE_BLK = 256
N_ASSIGN = N_TOK * TOP_K
N_BLOCKS = (N_ASSIGN + N_EXPERTS * (MOE_BLK - 1) + MOE_BLK - 1) // MOE_BLK
N_SLOTS = N_BLOCKS * MOE_BLK
HALF = D_MODEL // 2
COMB_TM = 128
VMEM_LIMIT = 56 * 1024 * 1024

_NT = (((1,), (1,)), ((), ()))
_TN = (((0,), (0,)), ((), ()))


def _nt(a, b, precision=None):
    return lax.dot_general(a, b, _NT, preferred_element_type=F32, precision=precision)


def _tn(a, b):
    return lax.dot_general(a, b, _TN, preferred_element_type=F32)


def _dot(a, b, precision=None):
    return jnp.dot(a, b, preferred_element_type=F32, precision=precision)


def _silu(x):
    return x * jax.nn.sigmoid(x)


def _mod_row(tile_start):
    return jnp.where(tile_start < NP_TOK, 0, 1 + (tile_start - NP_TOK) // DEC_SEQ)


def _params(sem, vmem=VMEM_LIMIT):
    return pltpu.CompilerParams(dimension_semantics=sem, vmem_limit_bytes=vmem)


def _pack_halves(y):
    a = lax.bitcast_convert_type(y[:, :HALF].astype(BF16).astype(F32), U32)
    b = lax.bitcast_convert_type(y[:, HALF:].astype(BF16).astype(F32), U32)
    return a | (b >> 16)


def _unpack_halves(u):
    a = lax.bitcast_convert_type(u & jnp.uint32(0xFFFF0000), F32)
    b = lax.bitcast_convert_type(u << 16, F32)
    return a, b


def _ada_kernel(c_ref, w_ref, b_ref, o_ref):
    a = _silu(c_ref[...]).astype(BF16)
    o_ref[...] = _dot(a, w_ref[...].astype(BF16)) + b_ref[...]


def _ada_mod(cvec, ada_w, ada_b):
    depth, d, n = ada_w.shape
    tn = 1024
    return pl.pallas_call(
        _ada_kernel,
        grid=(depth, n // tn),
        in_specs=[
            pl.BlockSpec((N_MOD, d), lambda l, j: (0, 0)),
            pl.BlockSpec((None, d, tn), lambda l, j: (l, 0, j)),
            pl.BlockSpec((None, 1, tn), lambda l, j: (l, 0, j)),
        ],
        out_specs=pl.BlockSpec((None, N_MOD, tn), lambda l, j: (l, 0, j)),
        out_shape=jax.ShapeDtypeStruct((depth, N_MOD, n), F32),
        compiler_params=_params(("parallel", "parallel")),
        name="ada_mod",
    )(cvec, ada_w, ada_b.reshape(depth, 1, n))


def _norm_mod(x, g, sc, sh):
    ms = jnp.mean(x * x, axis=-1, keepdims=True)
    return x * lax.rsqrt(ms + EPS) * g * (1.0 + sc) + sh


def _in_proj_kernel(x_ref, g_ref, sc_ref, sh_ref, w_ref, wab_ref, wab_lo_ref, o_ref, ab_ref, h_scr):
    @pl.when(pl.program_id(1) == 0)
    def _():
        hf = _norm_mod(x_ref[...], g_ref[...], sc_ref[...], sh_ref[...])
        h = hf.astype(BF16)
        h_scr[...] = h
        h_lo = (hf - h.astype(F32)).astype(BF16)
        ab_ref[...] = _dot(h, wab_ref[...]) + (_dot(h_lo, wab_ref[...]) + _dot(h, wab_lo_ref[...]))

    o_ref[...] = _dot(h_scr[...], w_ref[...]).astype(o_ref.dtype)


def _in_proj(x, g, sc, sh, w, wab_f32):
    wab = wab_f32.astype(BF16)
    wab_lo = (wab_f32 - wab.astype(F32)).astype(BF16)
    tm, tn = 1024, 1024
    n = w.shape[1]
    mod_spec = pl.BlockSpec((None, 1, D_MODEL), lambda i, j: (_mod_row(i * tm), 0, 0))
    return pl.pallas_call(
        _in_proj_kernel,
        grid=(N_TOK // tm, n // tn),
        in_specs=[
            pl.BlockSpec((tm, D_MODEL), lambda i, j: (i, 0)),
            pl.BlockSpec((1, D_MODEL), lambda i, j: (0, 0)),
            mod_spec,
            mod_spec,
            pl.BlockSpec((D_MODEL, tn), lambda i, j: (0, j)),
            pl.BlockSpec((D_MODEL, N_GATE), lambda i, j: (0, 0)),
            pl.BlockSpec((D_MODEL, N_GATE), lambda i, j: (0, 0)),
        ],
        out_specs=[
            pl.BlockSpec((tm, tn), lambda i, j: (i, j)),
            pl.BlockSpec((tm, N_GATE), lambda i, j: (i, 0)),
        ],
        out_shape=[
            jax.ShapeDtypeStruct((N_TOK, n), BF16),
            jax.ShapeDtypeStruct((N_TOK, N_GATE), F32),
        ],
        scratch_shapes=[pltpu.VMEM((tm, D_MODEL), BF16)],
        compiler_params=_params(("parallel", "arbitrary")),
        name="dn_in_proj",
    )(x, g, sc, sh, w, wab, wab_lo)


def _gates_kernel(ab_ref, alog_ref, dt_ref, o_ref):
    tm = ab_ref.shape[0]
    a = ab_ref[...]
    z = a + dt_ref[...]
    softplus = jnp.maximum(z, 0.0) + jnp.log(1.0 + jnp.exp(-jnp.abs(z)))
    g = -jnp.exp(alog_ref[...]) * softplus
    beta = jax.nn.sigmoid(a)
    ii = lax.broadcasted_iota(I32, (tm, tm), 0)
    jj = lax.broadcasted_iota(I32, (tm, tm), 1)
    same = (ii // CHUNK) == (jj // CHUNK)
    m_f = jnp.where(same & (jj <= ii), 1.0, 0.0).astype(F32)
    m_b = jnp.where(same & (jj >= ii), 1.0, 0.0).astype(F32)
    gc_f = _dot(m_f, g, precision=lax.Precision.HIGHEST)
    gc_b = _dot(m_b, g, precision=lax.Precision.HIGHEST)
    lane = lax.broadcasted_iota(I32, a.shape, 1)
    is_g = (lane % (2 * DN_NV)) < DN_NV
    o_ref[...] = jnp.where(is_g, jnp.where(lane < 2 * DN_NV, gc_f, gc_b), beta)


def _gates(ab, a_log, dt_bias):
    tm = 256
    zeros = jnp.zeros((DN_NV,), F32)
    alog = jnp.concatenate([a_log[0], zeros, a_log[1], zeros]).reshape(1, N_GATE)
    dt = jnp.concatenate([dt_bias[0], zeros, dt_bias[1], zeros]).reshape(1, N_GATE)
    return pl.pallas_call(
        _gates_kernel,
        grid=(N_TOK // tm,),
        in_specs=[
            pl.BlockSpec((tm, N_GATE), lambda i: (i, 0)),
            pl.BlockSpec((1, N_GATE), lambda i: (0, 0)),
            pl.BlockSpec((1, N_GATE), lambda i: (0, 0)),
        ],
        out_specs=pl.BlockSpec((tm, N_GATE), lambda i: (i, 0)),
        out_shape=jax.ShapeDtypeStruct((N_TOK, N_GATE), F32),
        compiler_params=_params(("parallel",)),
        name="dn_gates",
    )(ab, alog, dt)


CONV_ROWS = 2048
CONV_TC = 512


CONV_CHUNK = 128
CONV_EDGE = 16


def _conv_kernel(x_ref, w_ref, o_ref):
    i = pl.program_id(0)
    j = pl.program_id(1)
    rows = x_ref.shape[0]
    seq_len = jnp.where(i * rows < NP_TOK, SEQ, DEC_SEQ)
    half = DN_CONV // 2
    taps = [s for s in range(-half, half + 1) if s != 0]
    win_rows = CONV_CHUNK + 2 * CONV_EDGE
    q = lax.broadcasted_iota(I32, (len(taps) * CONV_CHUNK, win_rows), 0)
    col = lax.broadcasted_iota(I32, (len(taps) * CONV_CHUNK, win_rows), 1)
    src = CONV_EDGE + (q % CONV_CHUNK)
    for t, s in enumerate(taps):
        src = jnp.where(q // CONV_CHUNK == t, src + s, src)
    shift_mat = jnp.where(col == src, 1.0, 0.0).astype(BF16)
    is_qk = j < (2 * DN_QK) // CONV_TC
    qscale = jnp.where(j < DN_QK // CONV_TC, DN_DK ** -0.5, 1.0)

    def chunk(c, carry):
        r0 = pl.multiple_of(c * CONV_CHUNK, CONV_CHUNK)
        xc = x_ref[pl.ds(r0, CONV_CHUNK), :]
        r_prev = pl.multiple_of(jnp.maximum(r0 - CONV_EDGE, 0), CONV_EDGE)
        r_next = pl.multiple_of(jnp.minimum(r0 + CONV_CHUNK, rows - CONV_EDGE), CONV_EDGE)
        prev = x_ref[pl.ds(r_prev, CONV_EDGE), :]
        nxt = x_ref[pl.ds(r_next, CONV_EDGE), :]
        prev = jnp.where((r0 & (seq_len - 1)) != 0, prev, jnp.zeros_like(prev))
        nxt = jnp.where(((r0 + CONV_CHUNK) & (seq_len - 1)) != 0, nxt, jnp.zeros_like(nxt))
        shifted = _dot(shift_mat, jnp.concatenate([prev, xc, nxt], axis=0))
        acc = xc.astype(F32) * w_ref[half:half + 1, :]
        for t, s in enumerate(taps):
            acc = acc + shifted[t * CONV_CHUNK:(t + 1) * CONV_CHUNK] * w_ref[half + s:half + s + 1, :]
        y = _silu(acc)
        for h in range(CONV_TC // DN_DK):
            sl = slice(h * DN_DK, (h + 1) * DN_DK)
            yh = y[:, sl]
            inv = lax.rsqrt(jnp.sum(yh * yh, axis=-1, keepdims=True) + EPS) * qscale
            o_ref[pl.ds(r0, CONV_CHUNK), sl] = (yh * jnp.where(is_qk, inv, 1.0)).astype(o_ref.dtype)
        return carry

    lax.fori_loop(0, rows // CONV_CHUNK, chunk, 0, unroll=4)


def _conv_qkv(proj, conv_w):
    nch = 2 * DN_QK + DN_V
    return pl.pallas_call(
        _conv_kernel,
        grid=(N_TOK // CONV_ROWS, nch // CONV_TC),
        in_specs=[
            pl.BlockSpec((CONV_ROWS, CONV_TC), lambda i, j: (i, j)),
            pl.BlockSpec((DN_CONV, CONV_TC), lambda i, j: (0, j)),
        ],
        out_specs=pl.BlockSpec((CONV_ROWS, CONV_TC), lambda i, j: (i, j)),
        out_shape=jax.ShapeDtypeStruct((N_TOK, nch), BF16),
        compiler_params=_params(("parallel", "parallel")),
        name="dn_conv",
    )(proj, conv_w)


def _delta_kernel(*refs, n_chunks, has_s0, write_state):
    q_ref, k_ref, v_ref, gc_ref, rows_ref = refs[:5]
    pos = 5
    if has_s0:
        s0_ref = refs[pos]
        pos += 1
    o_ref = refs[pos]
    pos += 1
    if write_state:
        sfin_ref = refs[pos]
        pos += 1
    s_scr = refs[pos]
    out_ref = o_ref
    o_ref = refs[pos + 1]

    hk0 = pl.program_id(1) * DN_HKB
    if has_s0:
        for hl in range(DN_HKB):
            for d in range(2):
                for r in range(2):
                    s_scr[4 * hl + 2 * d + r] = s0_ref[d, 2 * hl + r]
    else:
        s_scr[...] = jnp.zeros(s_scr.shape, F32)
    o_ref[...] = jnp.zeros(o_ref.shape, F32)

    ii = lax.broadcasted_iota(I32, (CHUNK, CHUNK), 0)
    jj = lax.broadcasted_iota(I32, (CHUNK, CHUNK), 1)
    eye = jnp.where(ii == jj, 1.0, 0.0).astype(F32)
    lane = lax.broadcasted_iota(I32, (CHUNK, N_GATE), 1)
    pair_masks = [((ii >> (l + 1)) == (jj >> (l + 1))) & ((ii >> l) != (jj >> l))
                  for l in range(int(math.log2(CHUNK)))]

    chain_ids = [(hl, d, r) for hl in range(DN_HKB) for d in range(2) for r in range(2)]

    def chunk_index(t, d):
        return t if d == 0 else n_chunks - 1 - t

    def local_begin(t):
        chains = []
        for hl in range(DN_HKB):
            for d in range(2):
                c = chunk_index(t, d)
                r0 = pl.multiple_of(c * CHUNK, CHUNK)
                k = k_ref[pl.ds(r0, CHUNK), hl * DN_DK:(hl + 1) * DN_DK]
                q = q_ref[pl.ds(r0, CHUNK), hl * DN_DK:(hl + 1) * DN_DK]
                gates = gc_ref[pl.ds(r0, CHUNK), :]
                rows = rows_ref[hl, c]
                kk = _nt(k, k)
                qk = _nt(q, k)
                incl = (jj <= ii) if d == 0 else (jj >= ii)
                strict = (jj < ii) if d == 0 else (jj > ii)
                last = CHUNK - 1 if d == 0 else 0
                for r in range(2):
                    lane_g = d * 2 * DN_NV + 2 * (hk0 + hl) + r
                    gcc = jnp.sum(jnp.where(lane == lane_g, gates, 0.0), axis=1, keepdims=True)
                    bc = jnp.sum(jnp.where(lane == lane_g + DN_NV, gates, 0.0), axis=1, keepdims=True)
                    gcr = rows[2 * d + r:2 * d + r + 1, :]
                    gl = gcr[:, last:last + 1]
                    dec = jnp.where(incl, jnp.exp(jnp.where(incl, gcc - gcr, 0.0)), 0.0)
                    lm = jnp.where(strict, bc * kk * dec, 0.0)
                    v = v_ref[pl.ds(r0, CHUNK), (2 * hl + r) * DN_DV:(2 * hl + r + 1) * DN_DV]
                    chains.append(dict(k=k, q=q, v=v, gcc=gcc, bc=bc, gl=gl, attn=(qk * dec).astype(BF16),
                                       lb=lm.astype(BF16), tm=eye - jnp.where(pair_masks[0], lm, 0.0)))
        return chains

    def local_level(chains, lvl):
        for ch in chains:
            ch["tb"] = ch["tm"].astype(BF16)
            ch["bt"] = _dot(jnp.where(pair_masks[lvl], ch["lb"], jnp.zeros_like(ch["lb"])), ch["tb"])
        for ch in chains:
            ch["tm"] = ch["tm"] - _dot(ch["tb"], ch["bt"].astype(BF16))

    def local_finish(chains):
        for ci, ch in enumerate(chains):
            kf = ch["k"].astype(F32)
            egc = jnp.exp(ch["gcc"])
            rhs = jnp.concatenate([(ch["v"].astype(F32) * ch["bc"]).astype(BF16),
                                   (kf * (ch["bc"] * egc)).astype(BF16)], axis=1)
            ch["uw"] = _dot(ch["tm"].astype(BF16), rhs)
            ch["qg"] = (ch["q"].astype(F32) * egc).astype(BF16)
            ch["kd"] = (kf * jnp.exp(ch["gl"] - ch["gcc"])).astype(BF16)

    def scan_state_products(chains):
        for ci, ch in enumerate(chains):
            ch["s"] = s_scr[ci]
            wq = jnp.concatenate([ch["uw"][:, DN_DV:].astype(BF16), ch["qg"]], axis=0)
            ch["ws_qs"] = _dot(wq, ch["s"].astype(BF16))

    def scan_update(t, chains):
        for ci, (hl, d, r) in enumerate(chain_ids):
            ch = chains[ci]
            r0 = pl.multiple_of(chunk_index(t, d) * CHUNK, CHUNK)
            v_new = (ch["uw"][:, :DN_DV] - ch["ws_qs"][:CHUNK]).astype(BF16)
            o = ch["ws_qs"][CHUNK:] + _dot(ch["attn"], v_new)
            s_scr[ci] = ch["s"] * jnp.exp(ch["gl"]) + _tn(ch["kd"], v_new)
            oc = 2 * hl + r
            o_ref[pl.ds(r0, CHUNK), oc * DN_DV:(oc + 1) * DN_DV] += o

    def body(t, carry):
        chains = local_begin(t)
        for lvl in range(1, len(pair_masks)):
            local_level(chains, lvl)
        local_finish(chains)
        scan_state_products(chains)
        scan_update(t, chains)
        return carry

    lax.fori_loop(0, n_chunks, body, 0)

    def emit(c, carry):
        r0 = pl.multiple_of(c * CHUNK, CHUNK)
        out_ref[pl.ds(r0, CHUNK), :] = o_ref[pl.ds(r0, CHUNK), :].astype(out_ref.dtype)
        return carry

    lax.fori_loop(0, n_chunks, emit, 0)
    if write_state:
        for hl in range(DN_HKB):
            for d in range(2):
                for r in range(2):
                    sfin_ref[d, 2 * hl + r] = s_scr[4 * hl + 2 * d + r]


def _delta(qkv, gc, rows, s0, *, n_seq, seq_len, row_blk0, write_state):
    n_chunks = seq_len // CHUNK
    has_s0 = s0 is not None
    kern = functools.partial(_delta_kernel, n_chunks=n_chunks, has_s0=has_s0, write_state=write_state)
    ngrp = DN_NK // DN_HKB
    qk_w = DN_HKB * DN_DK
    v_w = 2 * DN_HKB * DN_DV
    in_specs = [
        pl.BlockSpec((seq_len, qk_w), lambda b, h: (row_blk0 + b, h)),
        pl.BlockSpec((seq_len, qk_w), lambda b, h: (row_blk0 + b, ngrp + h)),
        pl.BlockSpec((seq_len, v_w), lambda b, h: (row_blk0 + b, ngrp + h)),
        pl.BlockSpec((seq_len, N_GATE), lambda b, h: (row_blk0 + b, 0)),
        pl.BlockSpec((None, DN_HKB, n_chunks, 8, CHUNK), lambda b, h: (b, h, 0, 0, 0)),
    ]
    args = [qkv, qkv, qkv, gc, rows]
    state_spec = pl.BlockSpec((None, None, 2, 2 * DN_HKB, DN_DK, DN_DV), lambda b, h: (b, 0, 0, h, 0, 0))
    if has_s0:
        in_specs.append(state_spec)
        args.append(s0)
    out_specs = [pl.BlockSpec((seq_len, v_w), lambda b, h: (b, h))]
    out_shape = [jax.ShapeDtypeStruct((n_seq * seq_len, DN_V), BF16)]
    if write_state:
        out_specs.append(state_spec)
        out_shape.append(jax.ShapeDtypeStruct((n_seq, 1, 2, DN_NV, DN_DK, DN_DV), F32))
    return pl.pallas_call(
        kern,
        grid=(n_seq, ngrp),
        in_specs=in_specs,
        out_specs=out_specs,
        out_shape=out_shape,
        scratch_shapes=[pltpu.VMEM((4 * DN_HKB, DN_DK, DN_DV), F32), pltpu.VMEM((seq_len, v_w), F32)],
        compiler_params=_params(("parallel", "parallel")),
        name="dn_delta_%d" % seq_len,
    )(*args)


def _gate_rows(gc, n_seq, seq_len):
    n_chunks = seq_len // CHUNK
    g = gc.reshape(n_seq, n_chunks, CHUNK, 2, 2, DN_NK, 2)
    g = g.transpose(0, 5, 1, 4, 3, 6, 2)
    return g.reshape(n_seq, DN_NK, n_chunks, 8, CHUNK)


def _deltanet_core(x, g, sc1, sh1, w_in, conv_w, a_log, dt_bias, state_dn):
    nqkv = 2 * DN_QK + 2 * DN_V
    proj, ab = _in_proj(x, g, sc1, sh1, w_in[:, :nqkv].astype(BF16), w_in[:, nqkv:])
    gc = _gates(ab, a_log, dt_bias)
    qkv = _conv_qkv(proj, conv_w)
    o_p, new_state = _delta(qkv, gc, _gate_rows(gc[:NP_TOK], BATCH, SEQ), None,
                            n_seq=BATCH, seq_len=SEQ, row_blk0=0, write_state=True)
    (o_s,) = _delta(qkv, gc, _gate_rows(gc[NP_TOK:], DEC_BATCH, DEC_SEQ), state_dn,
                    n_seq=DEC_BATCH, seq_len=DEC_SEQ, row_blk0=NP_TOK // DEC_SEQ, write_state=False)
    return jnp.concatenate([o_p, o_s], axis=0), proj, new_state


def _dn_out_kernel(o_ref, z_ref, og_ref, w_ref, x_ref, g1_ref, out_ref, a_scr):
    @pl.when(pl.program_id(1) == 0)
    def _():
        for h in range(DN_NV):
            sl = slice(h * DN_DV, (h + 1) * DN_DV)
            o = o_ref[:, sl].astype(F32)
            z = z_ref[:, sl].astype(F32)
            on = o * lax.rsqrt(jnp.mean(o * o, axis=-1, keepdims=True) + EPS) * og_ref[...]
            a_scr[:, sl] = (on * _silu(z)).astype(BF16)

    out_ref[...] = x_ref[...] + g1_ref[...] * _dot(a_scr[...], w_ref[...])


def _dn_out(o, proj, onorm_g, w_out, x, g1):
    tm, tn = 256, D_MODEL
    zblk = (2 * DN_QK + DN_V) // DN_V
    return pl.pallas_call(
        _dn_out_kernel,
        grid=(N_TOK // tm, D_MODEL // tn),
        in_specs=[
            pl.BlockSpec((tm, DN_V), lambda i, j: (i, 0)),
            pl.BlockSpec((tm, DN_V), lambda i, j: (i, zblk)),
            pl.BlockSpec((1, DN_DV), lambda i, j: (0, 0)),
            pl.BlockSpec((DN_V, tn), lambda i, j: (0, j), pipeline_mode=pl.Buffered(1)),
            pl.BlockSpec((tm, tn), lambda i, j: (i, j)),
            pl.BlockSpec((None, 1, tn), lambda i, j: (_mod_row(i * tm), 0, j)),
        ],
        out_specs=pl.BlockSpec((tm, tn), lambda i, j: (i, j)),
        out_shape=jax.ShapeDtypeStruct((N_TOK, D_MODEL), F32),
        scratch_shapes=[pltpu.VMEM((tm, DN_V), BF16)],
        compiler_params=_params(("parallel", "arbitrary")),
        name="dn_out",
    )(o, proj, onorm_g, w_out, x, g1)


def _rstd_kernel(x_ref, o_ref):
    x = x_ref[...]
    o_ref[...] = lax.rsqrt(jnp.mean(x * x, axis=-1, keepdims=True) + EPS)


def _rstd(x):
    tm = 1024
    return pl.pallas_call(
        _rstd_kernel,
        grid=(N_TOK // tm,),
        in_specs=[pl.BlockSpec((tm, D_MODEL), lambda i: (i, 0))],
        out_specs=pl.BlockSpec((tm, 1), lambda i: (i, 0)),
        out_shape=jax.ShapeDtypeStruct((N_TOK, 1), F32),
        compiler_params=_params(("parallel",)),
        name="rstd",
    )(x)


POOL_ROWS = 2048


def _window_sum(x, pos, length, stride, w):
    rows = x.shape[0]
    m = w // 2

    def shifted(a, off):
        valid = (pos + off >= 0) & (pos + off < length)
        return jnp.where(valid, pltpu.roll(a, (-off * stride) % rows, 0), 0.0)

    fwd = x
    bwd = x
    step = 1
    while step < m:
        fwd = fwd + shifted(fwd, step)
        bwd = bwd + shifted(bwd, -step)
        step *= 2
    return fwd + shifted(bwd, -1)


def _window_count(pos, length, w):
    lo = jnp.maximum(pos - w // 2, 0)
    hi = jnp.minimum(pos + (w - w // 2) - 1, length - 1)
    return (hi - lo + 1).astype(F32)


def _pool_kernel(x_ref, rstd_ref, g_ref, sc_ref, sh_ref, w_ref, ps_ref, g1_ref, o_ref):
    i = pl.program_id(0)
    grp = pl.program_id(1)
    x = x_ref[...]
    h = x * rstd_ref[...] * g_ref[...] * (1.0 + sc_ref[...]) + sh_ref[...]
    ridx = lax.broadcasted_iota(I32, (POOL_ROWS, 1), 0)

    def finish(pooled):
        p = (pooled - h).astype(BF16)
        y = _dot(p, w_ref[...]) * ps_ref[...]
        o_ref[...] = x + g1_ref[...] * y

    for gi, w in enumerate(POOL_WINDOWS):
        @pl.when((grp == gi) & (i * POOL_ROWS < NP_TOK))
        def _(w=w):
            pos = ridx & (SEQ - 1)
            s = _window_sum(h, pos, SEQ, 1, w)
            finish(s / _window_count(pos, SEQ, w))

        @pl.when((grp == gi) & (i * POOL_ROWS >= NP_TOK))
        def _(w=w):
            col = ridx & (GRID_W - 1)
            row = ridx // GRID_W
            n_rows = POOL_ROWS // GRID_W
            s = _window_sum(h, col, GRID_W, 1, w)
            s = _window_sum(s, row, n_rows, GRID_W, w)
            finish(s / (_window_count(col, GRID_W, w) * _window_count(row, n_rows, w)))


def _pool_mixer(x, rstd, g, sc, sh, pool_w, pool_scale, g1):
    ngrp = len(POOL_WINDOWS)
    mod = lambda i, j: (_mod_row(i * POOL_ROWS), 0, j)
    return pl.pallas_call(
        _pool_kernel,
        grid=(N_TOK // POOL_ROWS, ngrp),
        in_specs=[
            pl.BlockSpec((POOL_ROWS, POOL_GROUP), lambda i, j: (i, j)),
            pl.BlockSpec((POOL_ROWS, 1), lambda i, j: (i, 0)),
            pl.BlockSpec((1, POOL_GROUP), lambda i, j: (0, j)),
            pl.BlockSpec((None, 1, POOL_GROUP), mod),
            pl.BlockSpec((None, 1, POOL_GROUP), mod),
            pl.BlockSpec((None, POOL_GROUP, POOL_GROUP), lambda i, j: (j, 0, 0)),
            pl.BlockSpec((1, POOL_GROUP), lambda i, j: (0, j)),
            pl.BlockSpec((None, 1, POOL_GROUP), mod),
        ],
        out_specs=pl.BlockSpec((POOL_ROWS, POOL_GROUP), lambda i, j: (i, j)),
        out_shape=jax.ShapeDtypeStruct((N_TOK, D_MODEL), F32),
        compiler_params=_params(("parallel", "parallel")),
        name="pool_mixer",
    )(x, rstd, g, sc, sh, pool_w, pool_scale, g1)


def _moe_pre_kernel(x_ref, g_ref, sc_ref, sh_ref, rw_ref, rb_ref, hp_ref, idx_ref, wt_ref):
    tm = x_ref.shape[0]
    h = _norm_mod(x_ref[...], g_ref[...], sc_ref[...], sh_ref[...])
    hp_ref[...] = _pack_halves(h)
    logits = _nt(rw_ref[...], h, precision=lax.Precision.HIGHEST)
    s = jax.nn.sigmoid(logits)
    choice = s + rb_ref[...]
    neg = -jnp.inf
    row8 = lax.broadcasted_iota(I32, (GROUP_SIZE, tm), 0)
    rowg = lax.broadcasted_iota(I32, (N_EXPERT_GROUPS, tm), 0)

    def first_max(vals, riota, n):
        m = jnp.max(vals, axis=0, keepdims=True)
        i = jnp.min(jnp.where(vals == m, riota, n), axis=0, keepdims=True)
        return m, i

    gscore = jnp.zeros((N_EXPERT_GROUPS, tm), F32)
    for g in range(N_EXPERT_GROUPS):
        cg = choice[g * GROUP_SIZE:(g + 1) * GROUP_SIZE, :]
        m1, i1 = first_max(cg, row8, GROUP_SIZE)
        m2 = jnp.max(jnp.where(row8 == i1, neg, cg), axis=0, keepdims=True)
        gscore = jnp.where(rowg == g, m1 + m2, gscore)
    gsel = jnp.zeros((N_EXPERT_GROUPS, tm), F32)
    for _ in range(TOPK_GROUPS):
        _, i = first_max(gscore, rowg, N_EXPERT_GROUPS)
        gsel = jnp.where(rowg == i, 1.0, gsel)
        gscore = jnp.where(rowg == i, neg, gscore)
    masked = jnp.concatenate(
        [jnp.where(gsel[g:g + 1, :] > 0.0, choice[g * GROUP_SIZE:(g + 1) * GROUP_SIZE, :], neg)
         for g in range(N_EXPERT_GROUPS)], axis=0)
    rowe = lax.broadcasted_iota(I32, (N_EXPERTS, tm), 0)
    rowk = lax.broadcasted_iota(I32, (TOP_K, tm), 0)
    idx = jnp.zeros((TOP_K, tm), I32)
    wts = jnp.zeros((TOP_K, tm), F32)
    for kk in range(TOP_K):
        _, i = first_max(masked, rowe, N_EXPERTS)
        sel = rowe == i
        wk = jnp.sum(jnp.where(sel, s, 0.0), axis=0, keepdims=True)
        idx = jnp.where(rowk == kk, i, idx)
        wts = jnp.where(rowk == kk, wk, wts)
        masked = jnp.where(sel, neg, masked)
    wts = wts / jnp.sum(wts, axis=0, keepdims=True) * ROUTED_SCALE
    idx_ref[...] = idx
    wt_ref[...] = wts


def _moe_pre(x, g, sc, sh, router_wt, router_b):
    tm = 256
    mod_spec = pl.BlockSpec((None, 1, D_MODEL), lambda i: (_mod_row(i * tm), 0, 0))
    return pl.pallas_call(
        _moe_pre_kernel,
        grid=(N_TOK // tm,),
        in_specs=[
            pl.BlockSpec((tm, D_MODEL), lambda i: (i, 0)),
            pl.BlockSpec((1, D_MODEL), lambda i: (0, 0)),
            mod_spec,
            mod_spec,
            pl.BlockSpec((N_EXPERTS, D_MODEL), lambda i: (0, 0)),
            pl.BlockSpec((N_EXPERTS, 1), lambda i: (0, 0)),
        ],
        out_specs=[
            pl.BlockSpec((tm, HALF), lambda i: (i, 0)),
            pl.BlockSpec((TOP_K, tm), lambda i: (0, i)),
            pl.BlockSpec((TOP_K, tm), lambda i: (0, i)),
        ],
        out_shape=[
            jax.ShapeDtypeStruct((N_TOK, HALF), U32),
            jax.ShapeDtypeStruct((TOP_K, N_TOK), I32),
            jax.ShapeDtypeStruct((TOP_K, N_TOK), F32),
        ],
        compiler_params=_params(("parallel",)),
        name="moe_pre",
    )(x, g, sc, sh, router_wt, router_b)


def _route(idx_t):
    flat_e = idx_t.T.reshape(-1)
    iota = jnp.arange(N_ASSIGN, dtype=I32)
    e_sorted, order = lax.sort_key_val(flat_e, iota)
    _, inv_order = lax.sort_key_val(order, iota)
    start = jnp.searchsorted(e_sorted, jnp.arange(N_EXPERTS, dtype=I32), side="left", method="scan").astype(I32)
    counts = jnp.concatenate([start[1:], jnp.full((1,), N_ASSIGN, I32)]) - start
    padded = (counts + MOE_BLK - 1) // MOE_BLK * MOE_BLK
    pend = jnp.cumsum(padded)
    pstart = pend - padded
    shift = pstart - start
    dshift = shift - jnp.concatenate([jnp.zeros((1,), I32), shift[:-1]])
    steps = jnp.zeros((N_ASSIGN,), I32).at[start].add(dshift, mode="drop")
    dest_sorted = iota + jnp.cumsum(steps)
    pos = dest_sorted[inv_order]
    block_start = jnp.arange(N_BLOCKS, dtype=I32) * MOE_BLK
    block_e = jnp.minimum(jnp.sum(pend[None, :] <= block_start[:, None], axis=1), N_EXPERTS - 1).astype(I32)
    slot = block_start[:, None] + jnp.arange(MOE_BLK, dtype=I32)[None, :]
    within = slot - pstart[block_e][:, None]
    valid = (within < counts[block_e][:, None]) & (slot < pend[N_EXPERTS - 1])
    src = jnp.clip(slot - shift[block_e][:, None], 0, N_ASSIGN - 1)
    slot_tok = jnp.where(valid, order[src] // TOP_K, 0).astype(I32)
    n_used = (pend[N_EXPERTS - 1] // MOE_BLK).astype(I32).reshape(1)
    return slot_tok.reshape(N_BLOCKS, MOE_BLK), pos.astype(I32), block_e, n_used


GATHER_DEPTH = 3
GATHER_IDX_ROWS = 4


def _gather_pipeline(step, n_steps, idx_hbm, src_hbm, idx_smem, bufs, sem_idx, sem_rows, n_rows):
    last = idx_hbm.shape[0] - 1
    nbuf = GATHER_DEPTH

    def idx_copy(blk, slot):
        return pltpu.make_async_copy(idx_hbm.at[jnp.minimum(blk, last)], idx_smem.at[slot], sem_idx.at[slot])

    def row_copy(slot, r, row):
        return pltpu.make_async_copy(src_hbm.at[pl.ds(row, 1)], bufs[slot].at[pl.ds(r, 1)], sem_rows.at[slot])

    def wait_rows(slot):
        pltpu.make_async_copy(src_hbm.at[pl.ds(0, n_rows)], bufs[slot], sem_rows.at[slot]).wait()

    @pl.when(step == 0)
    def _():
        for s in range(nbuf - 1):
            idx_copy(s, s).start()
            idx_copy(s, s).wait()

            def one(r, carry, s=s):
                row_copy(s, r, idx_smem[s, r]).start(priority=s % 2)
                return carry
            lax.fori_loop(0, n_rows, one, 0, unroll=8)
        idx_copy(nbuf - 1, nbuf - 1).start()

    def run(compute):
        for cur in range(nbuf):
            new = (cur + nbuf - 1) % nbuf

            @pl.when((step < n_steps) & (step % nbuf == cur))
            def _(cur=cur, new=new):
                idx_copy(step + nbuf - 1, new).wait()
                wait_rows(cur)
                for r in range(n_rows):
                    row_copy(new, r, idx_smem[new, r]).start(priority=r % 2)
                idx_copy(step + nbuf, cur).start()
                compute(bufs[cur])

                @pl.when(step == n_steps - 1)
                def _():
                    for s in range(1, nbuf):
                        wait_rows((cur + s) % nbuf)
                    idx_copy(last, cur).wait()

    return run


def _expert_kernel(be_ref, nu_ref, tok_hbm, hp_hbm, wgu_ref, wd_ref, y_ref, idx_smem, xbuf0, xbuf1, xbuf2,
                   wgu_bf, wd_bf, sem_idx, sem_rows):
    b = pl.program_id(0)
    nu = nu_ref[0]
    run = _gather_pipeline(b, nu, tok_hbm, hp_hbm, idx_smem, (xbuf0, xbuf1, xbuf2), sem_idx, sem_rows, MOE_BLK)

    @pl.when((b == 0) | (be_ref[b] != be_ref[jnp.maximum(b - 1, 0)]))
    def _():
        rows = 256

        def cast_gu(i, carry):
            r = pl.multiple_of(i * rows, rows)
            wgu_bf[pl.ds(r, rows), :] = wgu_ref[pl.ds(r, rows), :].astype(BF16)
            return carry
        lax.fori_loop(0, D_MODEL // rows, cast_gu, 0)

        def cast_d(i, carry):
            r = pl.multiple_of(i * rows, rows)
            wd_bf[pl.ds(r, rows), :] = wd_ref[pl.ds(r, rows), :].astype(BF16)
            return carry
        lax.fori_loop(0, D_EXPERT // rows, cast_d, 0)

    def compute(xbuf):
        xa, xb = _unpack_halves(xbuf[...])
        gu = _dot(xa.astype(BF16), wgu_bf[:HALF, :]) + _dot(xb.astype(BF16), wgu_bf[HALF:, :])
        act = (_silu(gu[:, :D_EXPERT]) * gu[:, D_EXPERT:]).astype(BF16)
        y_ref[...] = _pack_halves(_dot(act, wd_bf[...]))

    run(compute)

    @pl.when(b >= nu)
    def _():
        y_ref[...] = jnp.zeros(y_ref.shape, U32)


def _experts(slot_tok, block_e, n_used, hp, w_gu, w_down, layer):
    grid_spec = pltpu.PrefetchScalarGridSpec(
        num_scalar_prefetch=2,
        grid=(N_BLOCKS,),
        in_specs=[
            pl.BlockSpec(memory_space=pl.ANY),
            pl.BlockSpec(memory_space=pl.ANY),
            pl.BlockSpec((None, None, D_MODEL, 2 * D_EXPERT), lambda b, be, nu: (layer, be[b], 0, 0)),
            pl.BlockSpec((None, None, D_EXPERT, D_MODEL), lambda b, be, nu: (layer, be[b], 0, 0)),
        ],
        out_specs=pl.BlockSpec((MOE_BLK, HALF), lambda b, be, nu: (b, 0)),
        scratch_shapes=[
            pltpu.SMEM((GATHER_IDX_ROWS, MOE_BLK), I32),
            pltpu.VMEM((MOE_BLK, HALF), U32),
            pltpu.VMEM((MOE_BLK, HALF), U32),
            pltpu.VMEM((MOE_BLK, HALF), U32),
            pltpu.VMEM((D_MODEL, 2 * D_EXPERT), BF16),
            pltpu.VMEM((D_EXPERT, D_MODEL), BF16),
            pltpu.SemaphoreType.DMA((GATHER_DEPTH,)),
            pltpu.SemaphoreType.DMA((GATHER_DEPTH,)),
        ],
    )
    return pl.pallas_call(
        _expert_kernel,
        grid_spec=grid_spec,
        out_shape=jax.ShapeDtypeStruct((N_SLOTS, HALF), U32),
        compiler_params=_params(("arbitrary",)),
        name="moe_experts",
    )(block_e, n_used, slot_tok, hp, w_gu, w_down)


COMB_ROWS = COMB_TM * TOP_K


def _combine_kernel(pos_hbm, y_hbm, x_ref, hp_ref, wt_ref, g2_ref, sgu_ref, sd_ref, fg_ref, o_ref,
                    idx_smem, ybuf0, ybuf1, ybuf2, sem_idx, sem_rows, *, final_norm):
    run = _gather_pipeline(pl.program_id(0), pl.num_programs(0), pos_hbm, y_hbm, idx_smem, (ybuf0, ybuf1, ybuf2),
                           sem_idx, sem_rows, COMB_ROWS)

    def compute(ybuf):
        ha, hb = _unpack_halves(hp_ref[...])
        gu = _dot(ha.astype(BF16), sgu_ref[:HALF, :]) + _dot(hb.astype(BF16), sgu_ref[HALF:, :])
        act = (_silu(gu[:, :D_SHARED]) * gu[:, D_SHARED:]).astype(BF16)
        ffn = _dot(act, sd_ref[...])
        acc_a = ffn[:, :HALF]
        acc_b = ffn[:, HALF:]
        for k in range(TOP_K):
            ya, yb = _unpack_halves(ybuf[k * COMB_TM:(k + 1) * COMB_TM, :])
            wk = wt_ref[:, k:k + 1]
            acc_a = acc_a + wk * ya
            acc_b = acc_b + wk * yb
        xa = x_ref[:, :HALF] + g2_ref[:, :HALF] * acc_a
        xb = x_ref[:, HALF:] + g2_ref[:, HALF:] * acc_b
        if final_norm:
            ms = (jnp.sum(xa * xa, axis=-1, keepdims=True) + jnp.sum(xb * xb, axis=-1, keepdims=True)) / D_MODEL
            inv = lax.rsqrt(ms + EPS)
            xa = xa * inv * fg_ref[:, :HALF]
            xb = xb * inv * fg_ref[:, HALF:]
        o_ref[:, :HALF] = xa
        o_ref[:, HALF:] = xb

    run(compute)


def _combine(pos, y, x, hp, wts, g2, sh_gu, sh_down, final_g, *, final_norm, tok0=0, n_tok=N_TOK):
    n_tiles = n_tok // COMB_TM
    t0 = tok0 // COMB_TM
    pos_t = pos.reshape(N_TOK // COMB_TM, COMB_TM, TOP_K)[t0:t0 + n_tiles]
    pos_t = pos_t.transpose(0, 2, 1).reshape(n_tiles, COMB_ROWS)
    kern = functools.partial(_combine_kernel, final_norm=final_norm)
    return pl.pallas_call(
        kern,
        grid=(n_tiles,),
        in_specs=[
            pl.BlockSpec(memory_space=pl.ANY),
            pl.BlockSpec(memory_space=pl.ANY),
            pl.BlockSpec((COMB_TM, D_MODEL), lambda i: (t0 + i, 0)),
            pl.BlockSpec((COMB_TM, HALF), lambda i: (t0 + i, 0)),
            pl.BlockSpec((COMB_TM, TOP_K), lambda i: (t0 + i, 0)),
            pl.BlockSpec((None, 1, D_MODEL), lambda i: (_mod_row((t0 + i) * COMB_TM), 0, 0)),
            pl.BlockSpec((D_MODEL, 2 * D_SHARED), lambda i: (0, 0)),
            pl.BlockSpec((D_SHARED, D_MODEL), lambda i: (0, 0)),
            pl.BlockSpec((1, D_MODEL), lambda i: (0, 0)),
        ],
        out_specs=pl.BlockSpec((COMB_TM, D_MODEL), lambda i: (i, 0)),
        out_shape=jax.ShapeDtypeStruct((n_tok, D_MODEL), F32),
        scratch_shapes=[
            pltpu.SMEM((GATHER_IDX_ROWS, COMB_ROWS), I32),
            pltpu.VMEM((COMB_ROWS, HALF), U32),
            pltpu.VMEM((COMB_ROWS, HALF), U32),
            pltpu.VMEM((COMB_ROWS, HALF), U32),
            pltpu.SemaphoreType.DMA((GATHER_DEPTH,)),
            pltpu.SemaphoreType.DMA((GATHER_DEPTH,)),
        ],
        compiler_params=_params(("arbitrary",)),
        name="moe_combine",
    )(pos_t, y, x, hp, wts, g2, sh_gu, sh_down, final_g)


def _moe(x, g, sc, sh, g2, router_w, router_b, w_gu, w_down, layer, sh_gu, sh_down, final_g, *, final_norm):
    hp, idx_t, wt_t = _moe_pre(x, g, sc, sh, router_w.T, router_b.reshape(N_EXPERTS, 1))
    slot_tok, pos, block_e, n_used = _route(idx_t)
    y = _experts(slot_tok, block_e, n_used, hp, w_gu, w_down, layer)
    args = (pos, y, x, hp, wt_t.T, g2, sh_gu.astype(BF16), sh_down.astype(BF16), final_g)
    if not final_norm:
        return _combine(*args, final_norm=False)
    return (_combine(*args, final_norm=True, tok0=0, n_tok=NP_TOK),
            _combine(*args, final_norm=True, tok0=NP_TOK, n_tok=NS_TOK))


def kernel(x_prompt, x_sample, state_dn, c, c_ctx, ada_w, ada_b, norm_g, final_g, dn_w_in, dn_conv, dn_a_log,
           dn_dt_bias, dn_onorm_g, dn_w_out, pool_w, pool_scale, moe_router, moe_bias, moe_w_gu, moe_w_down,
           sh_w_gu, sh_w_down):
    x = jnp.concatenate([x_prompt.reshape(NP_TOK, D_MODEL), x_sample.reshape(NS_TOK, D_MODEL)], axis=0)
    cvec = jnp.concatenate([c_ctx[None, :], c, jnp.zeros((N_MOD - 1 - DEC_BATCH, D_MODEL), F32)], axis=0)
    mod = _ada_mod(cvec, ada_w, ada_b)
    mod = mod.reshape(mod.shape[0], N_MOD, 6, 1, D_MODEL)
    fg = final_g.reshape(1, D_MODEL)

    def mods(layer):
        return [mod[layer, :, p] for p in range(6)]

    sh1, sc1, g1, sh2, sc2, g2 = mods(0)
    o, proj, new_state = _deltanet_core(x, norm_g[0, 0].reshape(1, D_MODEL), sc1, sh1, dn_w_in[0], dn_conv[0],
                                        dn_a_log[0], dn_dt_bias[0], state_dn)
    x = _dn_out(o, proj, dn_onorm_g[0].reshape(1, DN_DV), dn_w_out[0].astype(BF16), x, g1)
    x = _moe(x, norm_g[0, 1].reshape(1, D_MODEL), sc2, sh2, g2, moe_router[0], moe_bias[0], moe_w_gu,
             moe_w_down, 0, sh_w_gu[0], sh_w_down[0], fg, final_norm=False)

    sh1, sc1, g1, sh2, sc2, g2 = mods(1)
    x = _pool_mixer(x, _rstd(x), norm_g[1, 0].reshape(1, D_MODEL), sc1, sh1, pool_w[0].astype(BF16),
                    pool_scale[0].reshape(1, D_MODEL), g1)
    yp, ys = _moe(x, norm_g[1, 1].reshape(1, D_MODEL), sc2, sh2, g2, moe_router[1], moe_bias[1], moe_w_gu,
                  moe_w_down, 1, sh_w_gu[1], sh_w_down[1], fg, final_norm=True)
    return (yp.reshape(BATCH, SEQ, D_MODEL), ys.reshape(DEC_BATCH, DEC_SEQ, D_MODEL), new_state)
```
